```python
import math
import jax, jax.numpy as jnp
from jax import lax
import numpy as np

D_MODEL = 2048
BATCH = 2
SEQ = 8192
DEPTH = 4

CHUNK = 64
Q_BLOCK = 128
D_MIX = D_MODEL
D_ATT = D_MIX // 2
D_SSM = D_MIX - D_ATT
N_HEADS = 8
HEAD_DV = D_ATT // N_HEADS
HEAD_DK = HEAD_DV // 2
SSM_GROUP = 16
N_GROUPS = D_SSM // SSM_GROUP
STATE = 64
N_BUCKETS = 32
MAX_DISTANCE = 128
EPS = 1e-6
NEG_INF = -1e30
D_IN = 4 * D_ATT + 2 * D_SSM

kernel_name = "hybrid_diffattn_s5_parallel_heads"


def lambda_init_fn(layer_idx):
    return 0.8 - 0.6 * math.exp(-0.3 * layer_idx)


def rmsnorm(x, g):
    xf = x.astype(jnp.float32)
    y = xf * lax.rsqrt(jnp.mean(xf * xf, axis=-1, keepdims=True) + EPS)
    return (y * g.astype(jnp.float32)).astype(x.dtype)


def t5_bucket(rel):
    half = N_BUCKETS // 2
    max_exact = half // 2
    n = jnp.abs(rel)
    nf = jnp.maximum(n, 1).astype(jnp.float32)
    large = max_exact + (jnp.log(nf / max_exact) / math.log(MAX_DISTANCE / max_exact)
                         * (half - max_exact)).astype(jnp.int32)
    large = jnp.minimum(large, half - 1)
    return jnp.where(rel > 0, half, 0) + jnp.where(n < max_exact, n, large)


def diff_attention(q, k, v, lam, rel_bias):
    bsz, seq = q.shape[0], q.shape[1]
    scale = HEAD_DK ** -0.5
    k_pos = jnp.arange(seq, dtype=jnp.int32)
    k_chunk = k_pos // CHUNK
    table = rel_bias.astype(jnp.float32)

    def one_block(i):
        start = i * Q_BLOCK
        qb = lax.dynamic_slice_in_dim(q, start, Q_BLOCK, axis=1)
        q_pos = start + jnp.arange(Q_BLOCK, dtype=jnp.int32)
        logits = jnp.einsum("bqhmd,bkhmd->bmhqk", qb, k).astype(jnp.float32) * scale
        bias = jnp.transpose(table[t5_bucket(k_pos[None, :] - q_pos[:, None])], (2, 0, 1))
        allowed = k_chunk[None, :] <= (q_pos // CHUNK)[:, None]
        logits = jnp.where(allowed, logits + bias, NEG_INF)
        p = jax.nn.softmax(logits, axis=-1)
        w = p[:, 0] - lam * p[:, 1]
        return jnp.einsum("bhqk,bkhd->bqhd", w.astype(v.dtype), v)

    out = lax.map(one_block, jnp.arange(seq // Q_BLOCK, dtype=jnp.int32))
    return jnp.transpose(out, (1, 0, 2, 3, 4)).reshape(bsz, seq, N_HEADS, HEAD_DV)


def _diag_linear_combine(e1, e2):
    a1r, a1i, b1r, b1i = e1
    a2r, a2i, b2r, b2i = e2
    ar = a2r * a1r - a2i * a1i
    ai = a2r * a1i + a2i * a1r
    br = a2r * b1r - a2i * b1i + b2r
    bi = a2r * b1i + a2i * b1r + b2i
    return (ar, ai, br, bi)


def s5_ssm(u, a_re, a_im, log_dt, b_re, b_im, c_re, c_im, d_skip):
    f32 = jnp.float32
    bsz, seq = u.shape[0], u.shape[1]
    uf = u.astype(f32)
    ug = uf.reshape(bsz, seq, N_GROUPS, SSM_GROUP)
    a_re = a_re.astype(f32)
    a_im = a_im.astype(f32)
    b_re = b_re.astype(f32)
    b_im = b_im.astype(f32)
    dt = jnp.exp(log_dt.astype(f32))[:, None]
    mag = jnp.exp(dt * a_re)
    ab_re = mag * jnp.cos(dt * a_im)
    ab_im = mag * jnp.sin(dt * a_im)
    den = a_re * a_re + a_im * a_im
    nr = ab_re - 1.0
    cf_re = (nr * a_re + ab_im * a_im) / den
    cf_im = (ab_im * a_re - nr * a_im) / den
    bb_re = cf_re[..., None] * b_re - cf_im[..., None] * b_im
    bb_im = cf_re[..., None] * b_im + cf_im[..., None] * b_re
    bu_re = jnp.einsum("gpc,bsgc->bsgp", bb_re, ug)
    bu_im = jnp.einsum("gpc,bsgc->bsgp", bb_im, ug)
    at_re = jnp.broadcast_to(ab_re, (1, seq, N_GROUPS, STATE))
    at_im = jnp.broadcast_to(ab_im, (1, seq, N_GROUPS, STATE))
    _, _, h_re, h_im = lax.associative_scan(
        _diag_linear_combine, (at_re, at_im, bu_re, bu_im), axis=1)
    y = (jnp.einsum("gcp,bsgp->bsgc", c_re.astype(f32), h_re)
         - jnp.einsum("gcp,bsgp->bsgc", c_im.astype(f32), h_im))
    return y.reshape(bsz, seq, D_SSM) + d_skip.astype(f32) * uf


def setup_inputs(seed: int = 0) -> dict:
    key = jax.random.key(seed)
    ks = jax.random.split(key, 20)
    f32 = jnp.float32
    nrm = lambda k, shape, s: jax.random.normal(k, shape, f32) * s
    x = jax.random.normal(ks[0], (BATCH, SEQ, D_MODEL), f32)
    rel_bias = nrm(ks[1], (N_BUCKETS, N_HEADS), 0.1)
    pre_norm_g = 1.0 + nrm(ks[2], (DEPTH, D_MODEL), 0.02)
    post_norm_g = 1.0 + nrm(ks[3], (DEPTH, D_MODEL), 0.02)
    w_in = nrm(ks[4], (DEPTH, D_MODEL, D_IN), D_MODEL ** -0.5)
    lambda_q1 = nrm(ks[5], (DEPTH, HEAD_DK), 0.1)
    lambda_k1 = nrm(ks[6], (DEPTH, HEAD_DK), 0.1)
    lambda_q2 = nrm(ks[7], (DEPTH, HEAD_DK), 0.1)
    lambda_k2 = nrm(ks[8], (DEPTH, HEAD_DK), 0.1)
    subln_g = 1.0 + nrm(ks[9], (DEPTH, HEAD_DV), 0.02)
    ssm_a_re = -0.5 + nrm(ks[10], (DEPTH, N_GROUPS, STATE), 0.01)
    ssm_a_im = (math.pi * jnp.arange(STATE, dtype=f32))[None, None, :] + nrm(ks[11], (DEPTH, N_GROUPS, STATE), 0.01)
    ssm_log_dt = jax.random.uniform(ks[12], (DEPTH, N_GROUPS), f32, math.log(1e-3), math.log(1e-1))
    ssm_b_re = nrm(ks[13], (DEPTH, N_GROUPS, STATE, SSM_GROUP), (2 * SSM_GROUP) ** -0.5)
    ssm_b_im = nrm(ks[14], (DEPTH, N_GROUPS, STATE, SSM_GROUP), (2 * SSM_GROUP) ** -0.5)
    ssm_c_re = nrm(ks[15], (DEPTH, N_GROUPS, SSM_GROUP, STATE), (2 * STATE) ** -0.5)
    ssm_c_im = nrm(ks[16], (DEPTH, N_GROUPS, SSM_GROUP, STATE), (2 * STATE) ** -0.5)
    ssm_d = nrm(ks[17], (DEPTH, D_SSM), 1.0)
    w_glu = nrm(ks[18], (DEPTH, D_SSM, 2 * D_SSM), D_SSM ** -0.5)
    w_out = nrm(ks[19], (DEPTH, D_MIX, D_MODEL), D_MIX ** -0.5)
    return {"x": x, "rel_bias": rel_bias, "pre_norm_g": pre_norm_g, "post_norm_g": post_norm_g,
            "w_in": w_in, "lambda_q1": lambda_q1, "lambda_k1": lambda_k1,
            "lambda_q2": lambda_q2, "lambda_k2": lambda_k2, "subln_g": subln_g,
            "ssm_a_re": ssm_a_re, "ssm_a_im": ssm_a_im, "ssm_log_dt": ssm_log_dt,
            "ssm_b_re": ssm_b_re, "ssm_b_im": ssm_b_im, "ssm_c_re": ssm_c_re,
            "ssm_c_im": ssm_c_im, "ssm_d": ssm_d, "w_glu": w_glu, "w_out": w_out}


def reference(x, rel_bias, pre_norm_g, post_norm_g, w_in, lambda_q1, lambda_k1,
              lambda_q2, lambda_k2, subln_g, ssm_a_re, ssm_a_im, ssm_log_dt,
              ssm_b_re, ssm_b_im, ssm_c_re, ssm_c_im, ssm_d, w_glu, w_out):
    bsz, seq = x.shape[0], x.shape[1]
    split_at = [D_ATT, 2 * D_ATT, 3 * D_ATT, 4 * D_ATT, 4 * D_ATT + D_SSM]
    for l in range(DEPTH):
        h = rmsnorm(x, pre_norm_g[l])
        proj = jnp.einsum("bsd,de->bse", h, w_in[l])
        q, k, v, z_att, u, z_ssm = jnp.split(proj, split_at, axis=-1)

        lam_init = lambda_init_fn(l)
        lam = (jnp.exp(jnp.sum(lambda_q1[l].astype(jnp.float32) * lambda_k1[l].astype(jnp.float32)))
               - jnp.exp(jnp.sum(lambda_q2[l].astype(jnp.float32) * lambda_k2[l].astype(jnp.float32)))
               + lam_init)
        q = q.reshape(bsz, seq, N_HEADS, 2, HEAD_DK)
        k = k.reshape(bsz, seq, N_HEADS, 2, HEAD_DK)
        v = v.reshape(bsz, seq, N_HEADS, HEAD_DV)
        o_att = diff_attention(q, k, v, lam, rel_bias)
        o_att = rmsnorm(o_att, subln_g[l]) * (1.0 - lam_init)
        o_att = o_att.reshape(bsz, seq, D_ATT) * jax.nn.silu(z_att)

        y = s5_ssm(u, ssm_a_re[l], ssm_a_im[l], ssm_log_dt[l], ssm_b_re[l], ssm_b_im[l],
                   ssm_c_re[l], ssm_c_im[l], ssm_d[l]).astype(x.dtype)
        g = jnp.einsum("bsc,ce->bse", jax.nn.gelu(y), w_glu[l])
        g_val, g_gate = jnp.split(g, 2, axis=-1)
        o_ssm = g_val * jax.nn.sigmoid(g_gate) * jax.nn.silu(z_ssm)

        mix = jnp.einsum("bsc,cd->bsd", jnp.concatenate([o_att, o_ssm], axis=-1), w_out[l])
        x = x + rmsnorm(mix, post_norm_g[l])
    return x
```

```python
import functools
import math

import jax
import jax.numpy as jnp
from jax import lax
from jax.experimental import pallas as pl
from jax.experimental.pallas import tpu as pltpu

F32 = jnp.float32
BF16 = jnp.bfloat16

D_MODEL = 2048
DEPTH = 4
CHUNK = 64
D_ATT = 1024
D_SSM = 1024
N_HEADS = 8
HEAD_DV = 128
HEAD_DK = 64
SSM_GROUP = 16
N_GROUPS = 64
STATE = 64
N_BUCKETS = 32
MAX_DISTANCE = 128
EPS = 1e-6
NEG_INF = -1e30
D_IN = 4 * D_ATT + 2 * D_SSM

LANES = 128
SUBLANES = 8
MXU_DIM = 256
VMEM_LIMIT = 56 * 1024 * 1024

PROJ_TM = 1024
PROJ_TN = 1024
ATT_T = 256
SSM_TB = 256
SSM_PITCH = SSM_TB + SUBLANES
N_PAIRS = N_GROUPS // 2
OUT_TM = 512


def _lambda_init(layer_idx):
    return 0.8 - 0.6 * math.exp(-0.3 * layer_idx)


def _discretize_kernel(a_re_ref, a_im_ref, log_dt_ref, b_re_ref, b_im_ref,
                       ab_re_ref, ab_im_ref, bb_re_ref, bb_im_ref):
    a_re = a_re_ref[...]
    a_im = a_im_ref[...]
    dt = jnp.exp(log_dt_ref[...])
    mag = jnp.exp(dt * a_re)
    ab_re = mag * jnp.cos(dt * a_im)
    ab_im = mag * jnp.sin(dt * a_im)
    den = a_re * a_re + a_im * a_im
    nr = ab_re - 1.0
    cf_re = (nr * a_re + ab_im * a_im) / den
    cf_im = (ab_im * a_re - nr * a_im) / den
    b_re = b_re_ref[...]
    b_im = b_im_ref[...]
    ab_re_ref[...] = ab_re
    ab_im_ref[...] = ab_im
    bb_re_ref[...] = cf_re * b_re - cf_im * b_im
    bb_im_ref[...] = cf_re * b_im + cf_im * b_re


def _discretize(a_re, a_im, log_dt, b_re, b_im):
    nl = a_re.shape[0]
    a4 = (nl, N_GROUPS, 1, STATE)
    b4 = (nl, N_GROUPS, SSM_GROUP, STATE)
    return pl.pallas_call(
        _discretize_kernel,
        out_shape=(jax.ShapeDtypeStruct(a4, F32), jax.ShapeDtypeStruct(a4, F32),
                   jax.ShapeDtypeStruct(b4, F32), jax.ShapeDtypeStruct(b4, F32)),
        name="s5_discretize",
    )(a_re.reshape(a4), a_im.reshape(a4), log_dt.reshape(nl, N_GROUPS, 1, 1),
      jnp.swapaxes(b_re, 2, 3), jnp.swapaxes(b_im, 2, 3))


def _in_proj_kernel(x_ref, g_ref, w_ref, o_ref, h_ref):
    @pl.when(pl.program_id(1) == 0)
    def _():
        x = x_ref[...]
        y = x * lax.rsqrt(jnp.mean(x * x, axis=-1, keepdims=True) + EPS)
        h_ref[...] = (y * g_ref[...]).astype(BF16)

    o_ref[...] = jnp.dot(h_ref[...], w_ref[...],
                         preferred_element_type=F32).astype(o_ref.dtype)


def _in_proj(x2d, g, w):
    m = x2d.shape[0]
    return pl.pallas_call(
        _in_proj_kernel,
        grid=(m // PROJ_TM, D_IN // PROJ_TN),
        in_specs=[pl.BlockSpec((PROJ_TM, D_MODEL), lambda i, j: (i, 0)),
                  pl.BlockSpec((1, D_MODEL), lambda i, j: (0, 0)),
                  pl.BlockSpec((D_MODEL, PROJ_TN), lambda i, j: (0, j))],
        out_specs=pl.BlockSpec((PROJ_TM, PROJ_TN), lambda i, j: (i, j)),
        out_shape=jax.ShapeDtypeStruct((m, D_IN), BF16),
        scratch_shapes=[pltpu.VMEM((PROJ_TM, D_MODEL), BF16)],
        compiler_params=pltpu.CompilerParams(
            dimension_semantics=("arbitrary", "arbitrary"),
            vmem_limit_bytes=VMEM_LIMIT),
        name="in_proj",
    )(x2d, g, w)


def _attn_kernel(scal_ref, relb_ref, lq1_ref, lk1_ref, lq2_ref, lk2_ref, sg_ref,
                 q_ref, k_ref, v_ref, z_ref, o_ref,
                 bias_ref, m_ref, l_ref, acc_ref):
    t = ATT_T
    h = pl.program_id(1)
    qi = pl.program_id(2)

    @pl.when(qi == 0)
    def _():
        row = lax.broadcasted_iota(jnp.int32, (t, 2 * t), 0)
        col = lax.broadcasted_iota(jnp.int32, (t, 2 * t), 1) - t
        rel = col - row
        half = N_BUCKETS // 2
        max_exact = half // 2
        n = jnp.abs(rel)
        nf = jnp.maximum(n, 1).astype(F32)
        large = max_exact + (jnp.log(nf / max_exact) / math.log(MAX_DISTANCE / max_exact)
                             * (half - max_exact)).astype(jnp.int32)
        large = jnp.minimum(large, half - 1)
        bucket = jnp.where(rel > 0, half, 0) + jnp.where(n < max_exact, n, large)
        bias = jnp.zeros((t, 2 * t), F32)
        for i in range(N_BUCKETS):
            bias = jnp.where(bucket == i, relb_ref[i, h], bias)
        far = relb_ref[half - 1, h]
        allowed = (col < 0) | ((col // CHUNK) <= (row // CHUNK))
        bias_ref[...] = jnp.where(allowed, bias - far, NEG_INF)

    q = q_ref[...]
    lane = lax.broadcasted_iota(jnp.int32, (t, HEAD_DV), 1)
    zero = jnp.zeros_like(q)
    q_maps = (jnp.where(lane < HEAD_DK, q, zero), jnp.where(lane >= HEAD_DK, q, zero))

    def scores(mi, kj, bias):
        s = lax.dot_general(q_maps[mi], kj, (((1,), (1,)), ((), ())),
                            preferred_element_type=F32)
        return s if bias is None else s + bias

    def first_tile(start, bias):
        kj = k_ref[pl.ds(start, t), :]
        vj = v_ref[pl.ds(start, t), :]
        for mi in range(2):
            s = scores(mi, kj, bias)
            m_new = jnp.max(s, axis=1, keepdims=True)
            p = jnp.exp(s - m_new)
            m_ref[mi] = jnp.broadcast_to(m_new, (t, LANES))
            l_ref[mi] = jnp.broadcast_to(jnp.sum(p, axis=1, keepdims=True), (t, LANES))
            acc_ref[mi] = jnp.dot(p.astype(BF16), vj, preferred_element_type=F32)

    def next_tile(start, bias):
        kj = k_ref[pl.ds(start, t), :]
        vj = v_ref[pl.ds(start, t), :]
        for mi in range(2):
            s = scores(mi, kj, bias)
            m_prev = m_ref[mi]
            m_new = jnp.maximum(m_prev, jnp.max(s, axis=1, keepdims=True))
            alpha = jnp.exp(m_prev - m_new)
            p = jnp.exp(s - m_new[:, :1])
            m_ref[mi] = m_new
            l_ref[mi] = alpha * l_ref[mi] + jnp.sum(p, axis=1, keepdims=True)
            acc_ref[mi] = alpha * acc_ref[mi] + jnp.dot(p.astype(BF16), vj,
                                                        preferred_element_type=F32)

    q_start = pl.multiple_of(qi * t, t)
    first_tile(q_start, bias_ref[:, t:])

    @pl.when(qi >= 1)
    def _():
        next_tile(pl.multiple_of(q_start - t, t), bias_ref[:, :t])

    def far_body(j, carry):
        next_tile(pl.multiple_of(j * t, t), None)
        return carry

    lax.fori_loop(0, jnp.maximum(qi - 1, 0), far_body, 0)

    lam_init = scal_ref[0]
    lam = (jnp.exp(jnp.sum(lq1_ref[...] * lk1_ref[...], axis=1, keepdims=True))
           - jnp.exp(jnp.sum(lq2_ref[...] * lk2_ref[...], axis=1, keepdims=True))
           + lam_init)
    o = acc_ref[0] / l_ref[0] - lam * (acc_ref[1] / l_ref[1])
    o = o * lax.rsqrt(jnp.mean(o * o, axis=-1, keepdims=True) + EPS) * sg_ref[...]
    o = o * (1.0 - lam_init)
    z = z_ref[...].astype(F32)
    o_ref[...] = (o * (z * jax.nn.sigmoid(z))).astype(o_ref.dtype)


def _attention(proj, scal, rel_bias, lq1, lk1, lq2, lk2, subln_g):
    bsz, seq, _ = proj.shape
    t = ATT_T
    smem = pl.BlockSpec(memory_space=pltpu.SMEM)
    vec = lambda n: pl.BlockSpec((1, n), lambda b, h, i: (0, 0))
    return pl.pallas_call(
        _attn_kernel,
        grid=(bsz, N_HEADS, seq // t),
        in_specs=[smem, smem, vec(HEAD_DK), vec(HEAD_DK), vec(HEAD_DK), vec(HEAD_DK),
                  vec(HEAD_DV),
                  pl.BlockSpec((None, t, HEAD_DV), lambda b, h, i: (b, i, h)),
                  pl.BlockSpec((None, seq, HEAD_DV), lambda b, h, i: (b, 0, N_HEADS + h)),
                  pl.BlockSpec((None, seq, HEAD_DV), lambda b, h, i: (b, 0, 2 * N_HEADS + h)),
                  pl.BlockSpec((None, t, HEAD_DV), lambda b, h, i: (b, i, 3 * N_HEADS + h))],
        out_specs=pl.BlockSpec((None, t, HEAD_DV), lambda b, h, i: (b, i, h)),
        out_shape=jax.ShapeDtypeStruct((bsz, seq, D_ATT), BF16),
        scratch_shapes=[pltpu.VMEM((t, 2 * t), F32),
                        pltpu.VMEM((2, t, LANES), F32),
                        pltpu.VMEM((2, t, LANES), F32),
                        pltpu.VMEM((2, t, HEAD_DV), F32)],
        compiler_params=pltpu.CompilerParams(
            dimension_semantics=("arbitrary", "arbitrary", "arbitrary"),
            vmem_limit_bytes=VMEM_LIMIT),
        name="diff_attention",
    )(scal, rel_bias, lq1, lk1, lq2, lk2, subln_g, proj, proj, proj, proj)


def _gelu_tanh(x):
    return 0.5 * x * (1.0 + jnp.tanh(math.sqrt(2.0 / math.pi) * (x + 0.044715 * (x * x * x))))


def _ssm_kernel(u_ref, z_ref, bw_ref, cw_ref, are_ref, aim_ref, d_ref, wglu_ref,
                o_ref, hre_ref, him_ref, st_ref, g_ref):
    nb = u_ref.shape[0]
    tb = SSM_TB
    pitch = SSM_PITCH
    pairs_per_chunk = MXU_DIM // (2 * SSM_GROUP)
    n_vregs = N_PAIRS // SUBLANES

    @pl.when(pl.program_id(0) == 0)
    def _():
        st_ref[...] = jnp.zeros_like(st_ref)

    for b in range(nb):
        for gp in range(N_PAIRS):
            kc = gp // pairs_per_chunk
            uc = u_ref[b, :, kc * MXU_DIM:(kc + 1) * MXU_DIM]
            bu = jnp.dot(uc, bw_ref[gp], preferred_element_type=F32)
            base = (b * N_PAIRS + gp) * pitch
            hre_ref[base:base + tb, :] = bu[:, :LANES]
            him_ref[base:base + tb, :] = bu[:, LANES:]

    a_re = [are_ref[k] for k in range(n_vregs)]
    a_im = [aim_ref[k] for k in range(n_vregs)]

    def step(ti, carry):
        out = []
        for b in range(nb):
            for k in range(n_vregs):
                h_re, h_im = carry[2 * (b * n_vregs + k)], carry[2 * (b * n_vregs + k) + 1]
                rows = pl.ds((b * N_PAIRS + k * SUBLANES) * pitch + ti, SUBLANES, stride=pitch)
                n_re = a_re[k] * h_re - a_im[k] * h_im + hre_ref[rows, :]
                n_im = a_re[k] * h_im + a_im[k] * h_re + him_ref[rows, :]
                hre_ref[rows, :] = n_re
                him_ref[rows, :] = n_im
                out += [n_re, n_im]
        return tuple(out)

    init = tuple(st_ref[i] for i in range(2 * nb * n_vregs))
    final = lax.fori_loop(0, tb, step, init)
    for i, s in enumerate(final):
        st_ref[i] = s

    for b in range(nb):
        for kc in range(D_SSM // MXU_DIM):
            acc = None
            for j in range(pairs_per_chunk):
                base = (b * N_PAIRS + kc * pairs_per_chunk + j) * pitch
                hcat = jnp.concatenate([hre_ref[base:base + tb, :],
                                        him_ref[base:base + tb, :]], axis=1).astype(BF16)
                part = jnp.dot(hcat, cw_ref[kc * pairs_per_chunk + j],
                               preferred_element_type=F32)
                acc = part if acc is None else acc + part
            cols = slice(kc * MXU_DIM, (kc + 1) * MXU_DIM)
            y = acc + d_ref[:, cols] * u_ref[b, :, cols].astype(F32)
            g_ref[b * tb:(b + 1) * tb, cols] = _gelu_tanh(y).astype(BF16)

    glu = jnp.dot(g_ref[...], wglu_ref[...], preferred_element_type=F32)
    for b in range(nb):
        gb = glu[b * tb:(b + 1) * tb]
        z = z_ref[b].astype(F32)
        o = gb[:, :D_SSM] * jax.nn.sigmoid(gb[:, D_SSM:]) * (z * jax.nn.sigmoid(z))
        o_ref[b] = o.astype(o_ref.dtype)


def _ssm(proj, bw, cw, a_re_v, a_im_v, d_skip, w_glu):
    bsz, seq, _ = proj.shape
    tb = SSM_TB
    const = lambda shape: pl.BlockSpec(shape, lambda i: (0,) * len(shape),
                                       pipeline_mode=pl.Buffered(1))
    u_col = 4 * D_ATT // D_SSM
    n_state_vregs = 2 * bsz * (N_PAIRS // SUBLANES)
    return pl.pallas_call(
        _ssm_kernel,
        grid=(seq // tb,),
        in_specs=[pl.BlockSpec((bsz, tb, D_SSM), lambda i: (0, i, u_col)),
                  pl.BlockSpec((bsz, tb, D_SSM), lambda i: (0, i, u_col + 1)),
                  const((N_PAIRS, MXU_DIM, MXU_DIM)),
                  const((N_PAIRS, MXU_DIM, MXU_DIM)),
                  const((N_PAIRS // SUBLANES, SUBLANES, LANES)),
                  const((N_PAIRS // SUBLANES, SUBLANES, LANES)),
                  const((1, D_SSM)),
                  const((D_SSM, 2 * D_SSM))],
        out_specs=pl.BlockSpec((bsz, tb, D_SSM), lambda i: (0, i, 0)),
        out_shape=jax.ShapeDtypeStruct((bsz, seq, D_SSM), BF16),
        scratch_shapes=[pltpu.VMEM((bsz * N_PAIRS * SSM_PITCH, LANES), F32),
                        pltpu.VMEM((bsz * N_PAIRS * SSM_PITCH, LANES), F32),
                        pltpu.VMEM((n_state_vregs, SUBLANES, LANES), F32),
                        pltpu.VMEM((bsz * tb, D_SSM), BF16)],
        compiler_params=pltpu.CompilerParams(
            dimension_semantics=("arbitrary",),
            vmem_limit_bytes=VMEM_LIMIT),
        name="s5_glu",
    )(proj, proj, bw, cw, a_re_v, a_im_v, d_skip, w_glu)


def _ssm_weights(ab_re, ab_im, bb_re, bb_im, c_re, c_im):
    ppc = MXU_DIM // (2 * SSM_GROUP)
    sel = jax.nn.one_hot(jnp.arange(N_PAIRS) % ppc, ppc, dtype=F32)
    eye = jnp.eye(2, dtype=F32)
    shape5 = (2, N_PAIRS, 2, SSM_GROUP, STATE)
    bb = jnp.stack([bb_re, bb_im]).reshape(shape5)
    bw = jnp.einsum("rgkcp,gj,kl->gjkcrlp", bb, sel, eye)
    bw = bw.reshape(N_PAIRS, MXU_DIM, MXU_DIM).astype(BF16)
    cc = jnp.stack([c_re, -c_im]).reshape(shape5)
    cw = jnp.einsum("rgkcp,gj,kl->grkpjlc", cc, sel, eye)
    cw = cw.reshape(N_PAIRS, MXU_DIM, MXU_DIM).astype(BF16)
    vshape = (N_PAIRS // SUBLANES, SUBLANES, LANES)
    return bw, cw, ab_re.reshape(vshape), ab_im.reshape(vshape)


def _out_proj_kernel(oa_ref, os_ref, w_ref, g_ref, x_ref, o_ref):
    mix = (jnp.dot(oa_ref[...], w_ref[:D_ATT, :], preferred_element_type=F32)
           + jnp.dot(os_ref[...], w_ref[D_ATT:, :], preferred_element_type=F32))
    y = mix * lax.rsqrt(jnp.mean(mix * mix, axis=-1, keepdims=True) + EPS)
    o_ref[...] = x_ref[...] + y * g_ref[...]


def _out_proj(o_att, o_ssm, w, g, x2d):
    m = x2d.shape[0]
    return pl.pallas_call(
        _out_proj_kernel,
        grid=(m // OUT_TM,),
        in_specs=[pl.BlockSpec((OUT_TM, D_ATT), lambda i: (i, 0)),
                  pl.BlockSpec((OUT_TM, D_SSM), lambda i: (i, 0)),
                  pl.BlockSpec((D_MODEL, D_MODEL), lambda i: (0, 0),
                               pipeline_mode=pl.Buffered(1)),
                  pl.BlockSpec((1, D_MODEL), lambda i: (0, 0)),
                  pl.BlockSpec((OUT_TM, D_MODEL), lambda i: (i, 0))],
        out_specs=pl.BlockSpec((OUT_TM, D_MODEL), lambda i: (i, 0)),
        out_shape=jax.ShapeDtypeStruct((m, D_MODEL), F32),
        compiler_params=pltpu.CompilerParams(
            dimension_semantics=("arbitrary",),
            vmem_limit_bytes=VMEM_LIMIT),
        name="out_proj",
    )(o_att, o_ssm, w, g, x2d)


def kernel(x, rel_bias, pre_norm_g, post_norm_g, w_in, lambda_q1, lambda_k1, lambda_q2,
           lambda_k2, subln_g, ssm_a_re, ssm_a_im, ssm_log_dt, ssm_b_re, ssm_b_im,
           ssm_c_re, ssm_c_im, ssm_d, w_glu, w_out):
    bsz, seq, _ = x.shape
    m = bsz * seq
    ab_re, ab_im, bb_re, bb_im = _discretize(ssm_a_re, ssm_a_im, ssm_log_dt, ssm_b_re, ssm_b_im)
    col_scale = jnp.concatenate([jnp.full((D_ATT,), HEAD_DK ** -0.5, F32),
                                 jnp.ones((D_IN - D_ATT,), F32)])
    x2d = x.reshape(m, D_MODEL)
    for l in range(DEPTH):
        lam_init = _lambda_init(l)
        w_in_l = (w_in[l] * col_scale).astype(BF16)
        proj = _in_proj(x2d, pre_norm_g[l].reshape(1, D_MODEL), w_in_l)
        proj = proj.reshape(bsz, seq, D_IN)
        scal = jnp.array([lam_init], F32)
        o_att = _attention(proj, scal, rel_bias,
                           lambda_q1[l].reshape(1, HEAD_DK), lambda_k1[l].reshape(1, HEAD_DK),
                           lambda_q2[l].reshape(1, HEAD_DK), lambda_k2[l].reshape(1, HEAD_DK),
                           subln_g[l].reshape(1, HEAD_DV))
        bw, cw, a_re_v, a_im_v = _ssm_weights(
            ab_re[l, :, 0, :], ab_im[l, :, 0, :], bb_re[l], bb_im[l], ssm_c_re[l], ssm_c_im[l])
        o_ssm = _ssm(proj, bw, cw, a_re_v, a_im_v, ssm_d[l].reshape(1, D_SSM),
                     w_glu[l].astype(BF16))
        x2d = _out_proj(o_att.reshape(m, D_ATT), o_ssm.reshape(m, D_SSM),
                        w_out[l].astype(BF16), post_norm_g[l].reshape(1, D_MODEL), x2d)
    return x2d.reshape(bsz, seq, D_MODEL)
```

```python
import functools
import math

import jax
import jax.numpy as jnp
from jax import lax
from jax.experimental import pallas as pl
from jax.experimental.pallas import tpu as pltpu

F32 = jnp.float32
BF16 = jnp.bfloat16

D_MODEL = 2048
DEPTH = 4
CHUNK = 64
D_ATT = 1024
D_SSM = 1024
N_HEADS = 8
HEAD_DV = 128
HEAD_DK = 64
SSM_GROUP = 16
N_GROUPS = 64
STATE = 64
N_BUCKETS = 32
MAX_DISTANCE = 128
EPS = 1e-6
NEG_INF = -1e30
LOG2E = math.log2(math.e)
D_IN = 4 * D_ATT + 2 * D_SSM

LANES = 128
SUBLANES = 8
MXU_DIM = 256
VMEM_LIMIT = 56 * 1024 * 1024

PROJ_TM = 1024
PROJ_TN = 1024
ATT_T = 512
SSM_TB = 256
SSM_PITCH = SSM_TB + SUBLANES
N_PAIRS = N_GROUPS // 2
OUT_TM = 512


def _lambda_init(layer_idx):
    return 0.8 - 0.6 * math.exp(-0.3 * layer_idx)


def _discretize_kernel(a_re_ref, a_im_ref, log_dt_ref, b_re_ref, b_im_ref,
                       ab_re_ref, ab_im_ref, bb_re_ref, bb_im_ref):
    a_re = a_re_ref[...]
    a_im = a_im_ref[...]
    dt = jnp.exp(log_dt_ref[...])
    mag = jnp.exp(dt * a_re)
    ab_re = mag * jnp.cos(dt * a_im)
    ab_im = mag * jnp.sin(dt * a_im)
    den = a_re * a_re + a_im * a_im
    nr = ab_re - 1.0
    cf_re = (nr * a_re + ab_im * a_im) / den
    cf_im = (ab_im * a_re - nr * a_im) / den
    b_re = b_re_ref[...]
    b_im = b_im_ref[...]
    ab_re_ref[...] = ab_re
    ab_im_ref[...] = ab_im
    bb_re_ref[...] = cf_re * b_re - cf_im * b_im
    bb_im_ref[...] = cf_re * b_im + cf_im * b_re


def _discretize(a_re, a_im, log_dt, b_re, b_im):
    nl = a_re.shape[0]
    a4 = (nl, N_GROUPS, 1, STATE)
    b4 = (nl, N_GROUPS, SSM_GROUP, STATE)
    return pl.pallas_call(
        _discretize_kernel,
        out_shape=(jax.ShapeDtypeStruct(a4, F32), jax.ShapeDtypeStruct(a4, F32),
                   jax.ShapeDtypeStruct(b4, F32), jax.ShapeDtypeStruct(b4, F32)),
        name="s5_discretize",
    )(a_re.reshape(a4), a_im.reshape(a4), log_dt.reshape(nl, N_GROUPS, 1, 1),
      jnp.swapaxes(b_re, 2, 3), jnp.swapaxes(b_im, 2, 3))


def _in_proj_kernel(x_ref, g_ref, w_ref, o_ref, h_ref):
    @pl.when(pl.program_id(1) == 0)
    def _():
        x = x_ref[...]
        y = x * lax.rsqrt(jnp.mean(x * x, axis=-1, keepdims=True) + EPS)
        h_ref[...] = (y * g_ref[...]).astype(BF16)

    o_ref[...] = jnp.dot(h_ref[...], w_ref[...],
                         preferred_element_type=F32).astype(o_ref.dtype)


def _in_proj(x2d, g, w):
    m = x2d.shape[0]
    return pl.pallas_call(
        _in_proj_kernel,
        grid=(m // PROJ_TM, D_IN // PROJ_TN),
        in_specs=[pl.BlockSpec((PROJ_TM, D_MODEL), lambda i, j: (i, 0)),
                  pl.BlockSpec((1, D_MODEL), lambda i, j: (0, 0)),
                  pl.BlockSpec((D_MODEL, PROJ_TN), lambda i, j: (0, j))],
        out_specs=pl.BlockSpec((PROJ_TM, PROJ_TN), lambda i, j: (i, j)),
        out_shape=jax.ShapeDtypeStruct((m, D_IN), BF16),
        scratch_shapes=[pltpu.VMEM((PROJ_TM, D_MODEL), BF16)],
        compiler_params=pltpu.CompilerParams(
            dimension_semantics=("arbitrary", "arbitrary"),
            vmem_limit_bytes=VMEM_LIMIT),
        name="in_proj",
    )(x2d, g, w)


def _attn_kernel(scal_ref, relb_ref, lq1_ref, lk1_ref, lq2_ref, lk2_ref, sg_ref,
                 q_ref, k_ref, v_ref, z_ref, o_ref,
                 bias_ref, m_ref, l_ref, acc_ref):
    t = ATT_T
    n_lane_tiles = t // LANES
    h = pl.program_id(0)
    b = pl.program_id(1)
    qi = pl.program_id(2)

    @pl.when((b == 0) & (qi == 0))
    def _():
        w_rows, w_cols = LANES, 3 * LANES
        row = lax.broadcasted_iota(jnp.int32, (w_rows, w_cols), 0)
        col = lax.broadcasted_iota(jnp.int32, (w_rows, w_cols), 1) - LANES
        rel = col - row
        half = N_BUCKETS // 2
        max_exact = half // 2
        n = jnp.abs(rel)
        nf = jnp.maximum(n, 1).astype(F32)
        large = max_exact + (jnp.log(nf / max_exact) / math.log(MAX_DISTANCE / max_exact)
                             * (half - max_exact)).astype(jnp.int32)
        large = jnp.minimum(large, half - 1)
        bucket = jnp.where(rel > 0, half, 0) + jnp.where(n < max_exact, n, large)
        bias = jnp.zeros((w_rows, w_cols), F32)
        for i in range(N_BUCKETS):
            bias = jnp.where(bucket == i, relb_ref[i, h], bias)
        far = relb_ref[half - 1, h]
        allowed = ((col + LANES) // CHUNK - LANES // CHUNK) <= (row // CHUNK)
        window = jnp.where(allowed, (bias - far) * LOG2E, NEG_INF)
        for i in range(n_lane_tiles):
            rows = slice(i * LANES, (i + 1) * LANES)
            lo = (n_lane_tiles + i - 1) * LANES
            hi = min(lo + w_cols, 2 * t)
            bias_ref[rows, :lo] = jnp.zeros((LANES, lo), F32)
            bias_ref[rows, lo:hi] = window[:, :hi - lo]
            if hi < 2 * t:
                bias_ref[rows, hi:] = jnp.full((LANES, 2 * t - hi), NEG_INF, F32)

    q = q_ref[...]
    lane = lax.broadcasted_iota(jnp.int32, (t, HEAD_DV), 1)
    zero = jnp.zeros_like(q)
    q_maps = (jnp.where(lane < HEAD_DK, q, zero), jnp.where(lane >= HEAD_DK, q, zero))

    def scores(mi, kj, bias):
        s = lax.dot_general(q_maps[mi], kj, (((1,), (1,)), ((), ())),
                            preferred_element_type=F32)
        return s if bias is None else s + bias

    def first_tile(start, bias):
        kj = k_ref[pl.ds(start, t), :]
        vj = v_ref[pl.ds(start, t), :]
        for mi in range(2):
            s = scores(mi, kj, bias)
            m_new = jnp.max(s, axis=1, keepdims=True)
            p = jnp.exp2(s - m_new)
            m_ref[mi] = jnp.broadcast_to(m_new, (t, LANES))
            l_ref[mi] = jnp.broadcast_to(jnp.sum(p, axis=1, keepdims=True), (t, LANES))
            acc_ref[mi] = jnp.dot(p.astype(BF16), vj, preferred_element_type=F32)

    def next_tile(start, bias):
        kj = k_ref[pl.ds(start, t), :]
        vj = v_ref[pl.ds(start, t), :]
        for mi in range(2):
            s = scores(mi, kj, bias)
            m_prev = m_ref[mi]
            m_new = jnp.maximum(m_prev, jnp.max(s, axis=1, keepdims=True))
            alpha = jnp.exp2(m_prev - m_new)
            p = jnp.exp2(s - pltpu.repeat(m_new, n_lane_tiles, axis=1))
            m_ref[mi] = m_new
            l_ref[mi] = alpha * l_ref[mi] + jnp.sum(p, axis=1, keepdims=True)
            acc_ref[mi] = alpha * acc_ref[mi] + jnp.dot(p.astype(BF16), vj,
                                                        preferred_element_type=F32)

    q_start = pl.multiple_of(qi * t, t)
    first_tile(q_start, bias_ref[:, t:])

    @pl.when(qi >= 1)
    def _():
        next_tile(pl.multiple_of(q_start - t, t), bias_ref[:, :t])

    def far_body(j, carry):
        next_tile(pl.multiple_of(j * t, t), None)
        return carry

    lax.fori_loop(0, jnp.maximum(qi - 1, 0), far_body, 0)

    lam_init = scal_ref[0]
    lam = (jnp.exp(jnp.sum(lq1_ref[...] * lk1_ref[...], axis=1, keepdims=True))
           - jnp.exp(jnp.sum(lq2_ref[...] * lk2_ref[...], axis=1, keepdims=True))
           + lam_init)
    o = acc_ref[0] / l_ref[0] - lam * (acc_ref[1] / l_ref[1])
    o = o * lax.rsqrt(jnp.mean(o * o, axis=-1, keepdims=True) + EPS) * sg_ref[...]
    o = o * (1.0 - lam_init)
    z = z_ref[...].astype(F32)
    o_ref[...] = (o * (z * jax.nn.sigmoid(z))).astype(o_ref.dtype)


def _attention(proj, scal, rel_bias, lq1, lk1, lq2, lk2, subln_g):
    bsz, seq, _ = proj.shape
    t = ATT_T
    smem = pl.BlockSpec(memory_space=pltpu.SMEM)
    vec = lambda n: pl.BlockSpec((1, n), lambda h, b, i: (0, 0))
    return pl.pallas_call(
        _attn_kernel,
        grid=(N_HEADS, bsz, seq // t),
        in_specs=[smem, smem, vec(HEAD_DK), vec(HEAD_DK), vec(HEAD_DK), vec(HEAD_DK),
                  vec(HEAD_DV),
                  pl.BlockSpec((None, t, HEAD_DV), lambda h, b, i: (b, i, h)),
                  pl.BlockSpec((None, seq, HEAD_DV), lambda h, b, i: (b, 0, N_HEADS + h)),
                  pl.BlockSpec((None, seq, HEAD_DV), lambda h, b, i: (b, 0, 2 * N_HEADS + h)),
                  pl.BlockSpec((None, t, HEAD_DV), lambda h, b, i: (b, i, 3 * N_HEADS + h))],
        out_specs=pl.BlockSpec((None, t, HEAD_DV), lambda h, b, i: (b, i, h)),
        out_shape=jax.ShapeDtypeStruct((bsz, seq, D_ATT), BF16),
        scratch_shapes=[pltpu.VMEM((t, 2 * t), F32),
                        pltpu.VMEM((2, t, LANES), F32),
                        pltpu.VMEM((2, t, LANES), F32),
                        pltpu.VMEM((2, t, HEAD_DV), F32)],
        compiler_params=pltpu.CompilerParams(
            dimension_semantics=("arbitrary", "arbitrary", "arbitrary"),
            vmem_limit_bytes=VMEM_LIMIT),
        name="diff_attention",
    )(scal, rel_bias, lq1, lk1, lq2, lk2, subln_g, proj, proj, proj, proj)


def _gelu_tanh(x):
    return 0.5 * x * (1.0 + jnp.tanh(math.sqrt(2.0 / math.pi) * (x + 0.044715 * (x * x * x))))


def _ssm_kernel(u_ref, z_ref, bw_ref, cw_ref, are_ref, aim_ref, d_ref, wglu_ref,
                o_ref, hre_ref, him_ref, st_ref, g_ref):
    nb = u_ref.shape[0]
    tb = SSM_TB
    pitch = SSM_PITCH
    pairs_per_chunk = MXU_DIM // (2 * SSM_GROUP)
    n_vregs = N_PAIRS // SUBLANES

    @pl.when(pl.program_id(0) == 0)
    def _():
        st_ref[...] = jnp.zeros_like(st_ref)

    for b in range(nb):
        for gp in range(N_PAIRS):
            kc = gp // pairs_per_chunk
            uc = u_ref[b, :, kc * MXU_DIM:(kc + 1) * MXU_DIM]
            bu = jnp.dot(uc, bw_ref[gp], preferred_element_type=F32)
            base = (b * N_PAIRS + gp) * pitch
            hre_ref[base:base + tb, :] = bu[:, :LANES]
            him_ref[base:base + tb, :] = bu[:, LANES:]

    a_re = [are_ref[k] for k in range(n_vregs)]
    a_im = [aim_ref[k] for k in range(n_vregs)]

    def step(ti, carry):
        out = []
        for b in range(nb):
            for k in range(n_vregs):
                h_re, h_im = carry[2 * (b * n_vregs + k)], carry[2 * (b * n_vregs + k) + 1]
                rows = pl.ds((b * N_PAIRS + k * SUBLANES) * pitch + ti, SUBLANES, stride=pitch)
                n_re = a_re[k] * h_re - a_im[k] * h_im + hre_ref[rows, :]
                n_im = a_re[k] * h_im + a_im[k] * h_re + him_ref[rows, :]
                hre_ref[rows, :] = n_re
                him_ref[rows, :] = n_im
                out += [n_re, n_im]
        return tuple(out)

    init = tuple(st_ref[i] for i in range(2 * nb * n_vregs))
    final = lax.fori_loop(0, tb, step, init)
    for i, s in enumerate(final):
        st_ref[i] = s

    for b in range(nb):
        for kc in range(D_SSM // MXU_DIM):
            acc = None
            for j in range(pairs_per_chunk):
                base = (b * N_PAIRS + kc * pairs_per_chunk + j) * pitch
                hcat = jnp.concatenate([hre_ref[base:base + tb, :],
                                        him_ref[base:base + tb, :]], axis=1).astype(BF16)
                part = jnp.dot(hcat, cw_ref[kc * pairs_per_chunk + j],
                               preferred_element_type=F32)
                acc = part if acc is None else acc + part
            cols = slice(kc * MXU_DIM, (kc + 1) * MXU_DIM)
            y = acc + d_ref[:, cols] * u_ref[b, :, cols].astype(F32)
            g_ref[b * tb:(b + 1) * tb, cols] = _gelu_tanh(y).astype(BF16)

    glu = jnp.dot(g_ref[...], wglu_ref[...], preferred_element_type=F32)
    for b in range(nb):
        gb = glu[b * tb:(b + 1) * tb]
        z = z_ref[b].astype(F32)
        o = gb[:, :D_SSM] * jax.nn.sigmoid(gb[:, D_SSM:]) * (z * jax.nn.sigmoid(z))
        o_ref[b] = o.astype(o_ref.dtype)


def _ssm(proj, bw, cw, a_re_v, a_im_v, d_skip, w_glu):
    bsz, seq, _ = proj.shape
    tb = SSM_TB
    const = lambda shape: pl.BlockSpec(shape, lambda i: (0,) * len(shape),
                                       pipeline_mode=pl.Buffered(1))
    u_col = 4 * D_ATT // D_SSM
    n_state_vregs = 2 * bsz * (N_PAIRS // SUBLANES)
    return pl.pallas_call(
        _ssm_kernel,
        grid=(seq // tb,),
        in_specs=[pl.BlockSpec((bsz, tb, D_SSM), lambda i: (0, i, u_col)),
                  pl.BlockSpec((bsz, tb, D_SSM), lambda i: (0, i, u_col + 1)),
                  const((N_PAIRS, MXU_DIM, MXU_DIM)),
                  const((N_PAIRS, MXU_DIM, MXU_DIM)),
                  const((N_PAIRS // SUBLANES, SUBLANES, LANES)),
                  const((N_PAIRS // SUBLANES, SUBLANES, LANES)),
                  const((1, D_SSM)),
                  const((D_SSM, 2 * D_SSM))],
        out_specs=pl.BlockSpec((bsz, tb, D_SSM), lambda i: (0, i, 0)),
        out_shape=jax.ShapeDtypeStruct((bsz, seq, D_SSM), BF16),
        scratch_shapes=[pltpu.VMEM((bsz * N_PAIRS * SSM_PITCH, LANES), F32),
                        pltpu.VMEM((bsz * N_PAIRS * SSM_PITCH, LANES), F32),
                        pltpu.VMEM((n_state_vregs, SUBLANES, LANES), F32),
                        pltpu.VMEM((bsz * tb, D_SSM), BF16)],
        compiler_params=pltpu.CompilerParams(
            dimension_semantics=("arbitrary",),
            vmem_limit_bytes=VMEM_LIMIT),
        name="s5_glu",
    )(proj, proj, bw, cw, a_re_v, a_im_v, d_skip, w_glu)


def _ssm_weights(ab_re, ab_im, bb_re, bb_im, c_re, c_im):
    ppc = MXU_DIM // (2 * SSM_GROUP)
    sel = jax.nn.one_hot(jnp.arange(N_PAIRS) % ppc, ppc, dtype=F32)
    eye = jnp.eye(2, dtype=F32)
    shape5 = (2, N_PAIRS, 2, SSM_GROUP, STATE)
    bb = jnp.stack([bb_re, bb_im]).reshape(shape5)
    bw = jnp.einsum("rgkcp,gj,kl->gjkcrlp", bb, sel, eye)
    bw = bw.reshape(N_PAIRS, MXU_DIM, MXU_DIM).astype(BF16)
    cc = jnp.stack([c_re, -c_im]).reshape(shape5)
    cw = jnp.einsum("rgkcp,gj,kl->grkpjlc", cc, sel, eye)
    cw = cw.reshape(N_PAIRS, MXU_DIM, MXU_DIM).astype(BF16)
    vshape = (N_PAIRS // SUBLANES, SUBLANES, LANES)
    return bw, cw, ab_re.reshape(vshape), ab_im.reshape(vshape)


def _out_proj_kernel(oa_ref, os_ref, w_ref, g_ref, x_ref, o_ref):
    mix = (jnp.dot(oa_ref[...], w_ref[:D_ATT, :], preferred_element_type=F32)
           + jnp.dot(os_ref[...], w_ref[D_ATT:, :], preferred_element_type=F32))
    y = mix * lax.rsqrt(jnp.mean(mix * mix, axis=-1, keepdims=True) + EPS)
    o_ref[...] = x_ref[...] + y * g_ref[...]


def _out_proj(o_att, o_ssm, w, g, x2d):
    m = x2d.shape[0]
    return pl.pallas_call(
        _out_proj_kernel,
        grid=(m // OUT_TM,),
        in_specs=[pl.BlockSpec((OUT_TM, D_ATT), lambda i: (i, 0)),
                  pl.BlockSpec((OUT_TM, D_SSM), lambda i: (i, 0)),
                  pl.BlockSpec((D_MODEL, D_MODEL), lambda i: (0, 0),
                               pipeline_mode=pl.Buffered(1)),
                  pl.BlockSpec((1, D_MODEL), lambda i: (0, 0)),
                  pl.BlockSpec((OUT_TM, D_MODEL), lambda i: (i, 0))],
        out_specs=pl.BlockSpec((OUT_TM, D_MODEL), lambda i: (i, 0)),
        out_shape=jax.ShapeDtypeStruct((m, D_MODEL), F32),
        compiler_params=pltpu.CompilerParams(
            dimension_semantics=("arbitrary",),
            vmem_limit_bytes=VMEM_LIMIT),
        name="out_proj",
    )(o_att, o_ssm, w, g, x2d)


def kernel(x, rel_bias, pre_norm_g, post_norm_g, w_in, lambda_q1, lambda_k1, lambda_q2,
           lambda_k2, subln_g, ssm_a_re, ssm_a_im, ssm_log_dt, ssm_b_re, ssm_b_im,
           ssm_c_re, ssm_c_im, ssm_d, w_glu, w_out):
    bsz, seq, _ = x.shape
    m = bsz * seq
    ab_re, ab_im, bb_re, bb_im = _discretize(ssm_a_re, ssm_a_im, ssm_log_dt, ssm_b_re, ssm_b_im)
    col_scale = jnp.concatenate([jnp.full((D_ATT,), HEAD_DK ** -0.5 * LOG2E, F32),
                                 jnp.ones((D_IN - D_ATT,), F32)])
    x2d = x.reshape(m, D_MODEL)
    for l in range(DEPTH):
        lam_init = _lambda_init(l)
        w_in_l = (w_in[l] * col_scale).astype(BF16)
        proj = _in_proj(x2d, pre_norm_g[l].reshape(1, D_MODEL), w_in_l)
        proj = proj.reshape(bsz, seq, D_IN)
        scal = jnp.array([lam_init], F32)
        o_att = _attention(proj, scal, rel_bias,
                           lambda_q1[l].reshape(1, HEAD_DK), lambda_k1[l].reshape(1, HEAD_DK),
                           lambda_q2[l].reshape(1, HEAD_DK), lambda_k2[l].reshape(1, HEAD_DK),
                           subln_g[l].reshape(1, HEAD_DV))
        bw, cw, a_re_v, a_im_v = _ssm_weights(
            ab_re[l, :, 0, :], ab_im[l, :, 0, :], bb_re[l], bb_im[l], ssm_c_re[l], ssm_c_im[l])
        o_ssm = _ssm(proj, bw, cw, a_re_v, a_im_v, ssm_d[l].reshape(1, D_SSM),
                     w_glu[l].astype(BF16))
        x2d = _out_proj(o_att.reshape(m, D_ATT), o_ssm.reshape(m, D_SSM),
                        w_out[l].astype(BF16), post_norm_g[l].reshape(1, D_MODEL), x2d)
    return x2d.reshape(bsz, seq, D_MODEL)
```

```python
import functools
import math

import jax
import jax.numpy as jnp
from jax import lax
from jax.experimental import pallas as pl
from jax.experimental.pallas import tpu as pltpu

F32 = jnp.float32
BF16 = jnp.bfloat16

D_MODEL = 2048
DEPTH = 4
CHUNK = 64
D_ATT = 1024
D_SSM = 1024
N_HEADS = 8
HEAD_DV = 128
HEAD_DK = 64
SSM_GROUP = 16
N_GROUPS = 64
STATE = 64
N_BUCKETS = 32
MAX_DISTANCE = 128
EPS = 1e-6
NEG_INF = -1e30
LOG2E = math.log2(math.e)
D_IN = 4 * D_ATT + 2 * D_SSM

LANES = 128
SUBLANES = 8
MXU_DIM = 256
VMEM_LIMIT = 56 * 1024 * 1024

PROJ_TM = 1024
PROJ_TN = 1024
ATT_T = 512
SSM_TB = 256
SSM_PITCH = SSM_TB + SUBLANES
N_PAIRS = N_GROUPS // 2
OUT_TM = 512


def _lambda_init(layer_idx):
    return 0.8 - 0.6 * math.exp(-0.3 * layer_idx)


def _discretize_kernel(a_re_ref, a_im_ref, log_dt_ref, b_re_ref, b_im_ref,
                       ab_re_ref, ab_im_ref, bb_re_ref, bb_im_ref):
    a_re = a_re_ref[...]
    a_im = a_im_ref[...]
    dt = jnp.exp(log_dt_ref[...])
    mag = jnp.exp(dt * a_re)
    ab_re = mag * jnp.cos(dt * a_im)
    ab_im = mag * jnp.sin(dt * a_im)
    den = a_re * a_re + a_im * a_im
    nr = ab_re - 1.0
    cf_re = (nr * a_re + ab_im * a_im) / den
    cf_im = (ab_im * a_re - nr * a_im) / den
    b_re = b_re_ref[...]
    b_im = b_im_ref[...]
    ab_re_ref[...] = ab_re
    ab_im_ref[...] = ab_im
    bb_re_ref[...] = cf_re * b_re - cf_im * b_im
    bb_im_ref[...] = cf_re * b_im + cf_im * b_re


def _discretize(a_re, a_im, log_dt, b_re, b_im):
    nl = a_re.shape[0]
    a4 = (nl, N_GROUPS, 1, STATE)
    b4 = (nl, N_GROUPS, SSM_GROUP, STATE)
    return pl.pallas_call(
        _discretize_kernel,
        out_shape=(jax.ShapeDtypeStruct(a4, F32), jax.ShapeDtypeStruct(a4, F32),
                   jax.ShapeDtypeStruct(b4, F32), jax.ShapeDtypeStruct(b4, F32)),
        name="s5_discretize",
    )(a_re.reshape(a4), a_im.reshape(a4), log_dt.reshape(nl, N_GROUPS, 1, 1),
      jnp.swapaxes(b_re, 2, 3), jnp.swapaxes(b_im, 2, 3))


def _in_proj_kernel(x_ref, g_ref, w_ref, o_ref, h_ref):
    @pl.when(pl.program_id(1) == 0)
    def _():
        x = x_ref[...]
        y = x * lax.rsqrt(jnp.mean(x * x, axis=-1, keepdims=True) + EPS)
        h_ref[...] = (y * g_ref[...]).astype(BF16)

    o_ref[...] = jnp.dot(h_ref[...], w_ref[...],
                         preferred_element_type=F32).astype(o_ref.dtype)


def _in_proj(x2d, g, w):
    m = x2d.shape[0]
    return pl.pallas_call(
        _in_proj_kernel,
        grid=(m // PROJ_TM, D_IN // PROJ_TN),
        in_specs=[pl.BlockSpec((PROJ_TM, D_MODEL), lambda i, j: (i, 0)),
                  pl.BlockSpec((1, D_MODEL), lambda i, j: (0, 0)),
                  pl.BlockSpec((D_MODEL, PROJ_TN), lambda i, j: (0, j))],
        out_specs=pl.BlockSpec((PROJ_TM, PROJ_TN), lambda i, j: (i, j)),
        out_shape=jax.ShapeDtypeStruct((m, D_IN), BF16),
        scratch_shapes=[pltpu.VMEM((PROJ_TM, D_MODEL), BF16)],
        compiler_params=pltpu.CompilerParams(
            dimension_semantics=("arbitrary", "arbitrary"),
            vmem_limit_bytes=VMEM_LIMIT),
        name="in_proj",
    )(x2d, g, w)


def _attn_kernel(scal_ref, relb_ref, lq1_ref, lk1_ref, lq2_ref, lk2_ref, sg_ref,
                 q_ref, k_ref, v_ref, z_ref, o_ref,
                 bias_ref, m_ref, acc_ref, vx_ref, qm_ref, sa_ref, sb_ref, mxa_ref, mxb_ref):
    t = ATT_T
    n_lane_tiles = t // LANES
    h = pl.program_id(0)
    b = pl.program_id(1)
    qi = pl.program_id(2)

    @pl.when((b == 0) & (qi == 0))
    def _():
        w_rows, w_cols = LANES, 3 * LANES
        row = lax.broadcasted_iota(jnp.int32, (w_rows, w_cols), 0)
        col = lax.broadcasted_iota(jnp.int32, (w_rows, w_cols), 1) - LANES
        rel = col - row
        half = N_BUCKETS // 2
        max_exact = half // 2
        n = jnp.abs(rel)
        nf = jnp.maximum(n, 1).astype(F32)
        large = max_exact + (jnp.log(nf / max_exact) / math.log(MAX_DISTANCE / max_exact)
                             * (half - max_exact)).astype(jnp.int32)
        large = jnp.minimum(large, half - 1)
        bucket = jnp.where(rel > 0, half, 0) + jnp.where(n < max_exact, n, large)
        bias = jnp.zeros((w_rows, w_cols), F32)
        for i in range(N_BUCKETS):
            bias = jnp.where(bucket == i, relb_ref[i, h], bias)
        far = relb_ref[half - 1, h]
        allowed = ((col + LANES) // CHUNK - LANES // CHUNK) <= (row // CHUNK)
        window = jnp.where(allowed, (bias - far) * LOG2E, NEG_INF)
        for i in range(n_lane_tiles):
            rows = slice(i * LANES, (i + 1) * LANES)
            lo = (n_lane_tiles + i - 1) * LANES
            hi = min(lo + w_cols, 2 * t)
            bias_ref[rows, :lo] = jnp.zeros((LANES, lo), F32)
            bias_ref[rows, lo:hi] = window[:, :hi - lo]
            if hi < 2 * t:
                bias_ref[rows, hi:] = jnp.full((LANES, 2 * t - hi), NEG_INF, F32)

    @pl.when(qi == 0)
    def _():
        vx_ref[:, :HEAD_DV] = v_ref[...]
        vx_ref[:, HEAD_DV:] = jnp.ones((v_ref.shape[0], HEAD_DV), BF16)

    q = q_ref[...]
    lane = lax.broadcasted_iota(jnp.int32, (t, HEAD_DV), 1)
    zero = jnp.zeros_like(q)
    qm_ref[0] = jnp.where(lane < HEAD_DK, q, zero)
    qm_ref[1] = jnp.where(lane >= HEAD_DK, q, zero)

    def qk_scores(s_ref, mx_ref, tile, bias=None):
        kj = k_ref[pl.ds(pl.multiple_of(tile * t, t), t), :]
        if bias is None:
            corner = jnp.where(tile == qi - 1, bias_ref[:LANES, t - LANES:t], 0.0)
        for mi in range(2):
            s = lax.dot_general(qm_ref[mi], kj, (((1,), (1,)), ((), ())),
                                preferred_element_type=F32)
            if bias is None:
                top = jnp.concatenate([s[:LANES, :t - LANES], s[:LANES, t - LANES:] + corner],
                                      axis=1)
                s = jnp.concatenate([top, s[LANES:]], axis=0)
            else:
                s = s + bias
            s_ref[mi] = s
            mx_ref[mi] = jnp.broadcast_to(jnp.max(s, axis=1, keepdims=True), (t, LANES))

    def first_tile(s_ref, mx_ref, tile):
        vj = vx_ref[pl.ds(pl.multiple_of(tile * t, t), t), :]
        for mi in range(2):
            m_new = mx_ref[mi]
            p = jnp.exp2(s_ref[mi] - jnp.concatenate([m_new] * n_lane_tiles, axis=1))
            m_ref[mi] = m_new
            acc_ref[mi] = jnp.dot(p.astype(BF16), vj, preferred_element_type=F32)

    def next_tile(s_ref, mx_ref, tile):
        vj = vx_ref[pl.ds(pl.multiple_of(tile * t, t), t), :]
        for mi in range(2):
            m_prev = m_ref[mi]
            m_new = jnp.maximum(m_prev, mx_ref[mi])
            alpha = jnp.exp2(m_prev - m_new)
            p = jnp.exp2(s_ref[mi] - jnp.concatenate([m_new] * n_lane_tiles, axis=1))
            m_ref[mi] = m_new
            acc_ref[mi] = (jnp.concatenate([alpha, alpha], axis=1) * acc_ref[mi]
                           + jnp.dot(p.astype(BF16), vj, preferred_element_type=F32))

    n_prev = qi
    last = jnp.maximum(n_prev - 1, 0)
    qk_scores(sa_ref, mxa_ref, qi, bias_ref[:, t:])
    qk_scores(sb_ref, mxb_ref, 0)
    first_tile(sa_ref, mxa_ref, qi)

    def pair_body(i, carry):
        qk_scores(sa_ref, mxa_ref, jnp.minimum(2 * i + 1, last))
        next_tile(sb_ref, mxb_ref, 2 * i)
        qk_scores(sb_ref, mxb_ref, jnp.minimum(2 * i + 2, last))
        next_tile(sa_ref, mxa_ref, 2 * i + 1)
        return carry

    lax.fori_loop(0, n_prev // 2, pair_body, 0)

    @pl.when(n_prev % 2 == 1)
    def _():
        next_tile(sb_ref, mxb_ref, last)

    lam_init = scal_ref[0]
    lam = (jnp.exp(jnp.sum(lq1_ref[...] * lk1_ref[...], axis=1, keepdims=True))
           - jnp.exp(jnp.sum(lq2_ref[...] * lk2_ref[...], axis=1, keepdims=True))
           + lam_init)
    o = (acc_ref[0, :, :HEAD_DV] / acc_ref[0, :, HEAD_DV:]
         - lam * (acc_ref[1, :, :HEAD_DV] / acc_ref[1, :, HEAD_DV:]))
    o = o * lax.rsqrt(jnp.mean(o * o, axis=-1, keepdims=True) + EPS) * sg_ref[...]
    o = o * (1.0 - lam_init)
    z = z_ref[...].astype(F32)
    o_ref[...] = (o * (z * jax.nn.sigmoid(z))).astype(o_ref.dtype)


def _attention(proj, scal, rel_bias, lq1, lk1, lq2, lk2, subln_g):
    bsz, seq, _ = proj.shape
    t = ATT_T
    smem = pl.BlockSpec(memory_space=pltpu.SMEM)
    vec = lambda n: pl.BlockSpec((1, n), lambda h, b, i: (0, 0))
    return pl.pallas_call(
        _attn_kernel,
        grid=(N_HEADS, bsz, seq // t),
        in_specs=[smem, smem, vec(HEAD_DK), vec(HEAD_DK), vec(HEAD_DK), vec(HEAD_DK),
                  vec(HEAD_DV),
                  pl.BlockSpec((None, t, HEAD_DV), lambda h, b, i: (b, i, h)),
                  pl.BlockSpec((None, seq, HEAD_DV), lambda h, b, i: (b, 0, N_HEADS + h)),
                  pl.BlockSpec((None, seq, HEAD_DV), lambda h, b, i: (b, 0, 2 * N_HEADS + h)),
                  pl.BlockSpec((None, t, HEAD_DV), lambda h, b, i: (b, i, 3 * N_HEADS + h))],
        out_specs=pl.BlockSpec((None, t, HEAD_DV), lambda h, b, i: (b, i, h)),
        out_shape=jax.ShapeDtypeStruct((bsz, seq, D_ATT), BF16),
        scratch_shapes=[pltpu.VMEM((t, 2 * t), F32),
                        pltpu.VMEM((2, t, LANES), F32),
                        pltpu.VMEM((2, t, 2 * HEAD_DV), F32),
                        pltpu.VMEM((seq, 2 * HEAD_DV), BF16),
                        pltpu.VMEM((2, t, HEAD_DV), BF16),
                        pltpu.VMEM((2, t, t), F32),
                        pltpu.VMEM((2, t, t), F32),
                        pltpu.VMEM((2, t, LANES), F32),
                        pltpu.VMEM((2, t, LANES), F32)],
        compiler_params=pltpu.CompilerParams(
            dimension_semantics=("arbitrary", "arbitrary", "arbitrary"),
            vmem_limit_bytes=VMEM_LIMIT),
        name="diff_attention",
    )(scal, rel_bias, lq1, lk1, lq2, lk2, subln_g, proj, proj, proj, proj)


def _gelu_tanh(x):
    return 0.5 * x * (1.0 + jnp.tanh(math.sqrt(2.0 / math.pi) * (x + 0.044715 * (x * x * x))))


def _ssm_kernel(u_ref, z_ref, bw_ref, cw_ref, are_ref, aim_ref, d_ref, wglu_ref,
                o_ref, hre_ref, him_ref, st_ref, g_ref):
    nb = u_ref.shape[0]
    tb = SSM_TB
    pitch = SSM_PITCH
    pairs_per_chunk = MXU_DIM // (2 * SSM_GROUP)
    n_vregs = N_PAIRS // SUBLANES

    @pl.when(pl.program_id(0) == 0)
    def _():
        st_ref[...] = jnp.zeros_like(st_ref)

    for gp in range(N_PAIRS):
        kc = gp // pairs_per_chunk
        uc = jnp.concatenate([u_ref[b, :, kc * MXU_DIM:(kc + 1) * MXU_DIM] for b in range(nb)],
                             axis=0)
        bu = jnp.dot(uc, bw_ref[gp], preferred_element_type=F32)
        for b in range(nb):
            base = (b * N_PAIRS + gp) * pitch
            hre_ref[base:base + tb, :] = bu[b * tb:(b + 1) * tb, :LANES]
            him_ref[base:base + tb, :] = bu[b * tb:(b + 1) * tb, LANES:]

    a_re = [are_ref[k] for k in range(n_vregs)]
    a_im = [aim_ref[k] for k in range(n_vregs)]

    def step(ti, carry):
        out = []
        for b in range(nb):
            for k in range(n_vregs):
                h_re, h_im = carry[2 * (b * n_vregs + k)], carry[2 * (b * n_vregs + k) + 1]
                rows = pl.ds((b * N_PAIRS + k * SUBLANES) * pitch + ti, SUBLANES, stride=pitch)
                n_re = a_re[k] * h_re - a_im[k] * h_im + hre_ref[rows, :]
                n_im = a_re[k] * h_im + a_im[k] * h_re + him_ref[rows, :]
                hre_ref[rows, :] = n_re
                him_ref[rows, :] = n_im
                out += [n_re, n_im]
        return tuple(out)

    init = tuple(st_ref[i] for i in range(2 * nb * n_vregs))
    final = lax.fori_loop(0, tb, step, init, unroll=2)
    for i, s in enumerate(final):
        st_ref[i] = s

    for kc in range(D_SSM // MXU_DIM):
        acc = None
        for j in range(pairs_per_chunk):
            gp = kc * pairs_per_chunk + j
            hcat = jnp.concatenate(
                [jnp.concatenate([hre_ref[(b * N_PAIRS + gp) * pitch:(b * N_PAIRS + gp) * pitch + tb, :],
                                  him_ref[(b * N_PAIRS + gp) * pitch:(b * N_PAIRS + gp) * pitch + tb, :]],
                                 axis=1) for b in range(nb)], axis=0).astype(BF16)
            part = jnp.dot(hcat, cw_ref[gp], preferred_element_type=F32)
            acc = part if acc is None else acc + part
        cols = slice(kc * MXU_DIM, (kc + 1) * MXU_DIM)
        for b in range(nb):
            y = acc[b * tb:(b + 1) * tb] + d_ref[:, cols] * u_ref[b, :, cols].astype(F32)
            g_ref[b * tb:(b + 1) * tb, cols] = _gelu_tanh(y).astype(BF16)

    glu = jnp.dot(g_ref[...], wglu_ref[...], preferred_element_type=F32)
    for b in range(nb):
        gb = glu[b * tb:(b + 1) * tb]
        z = z_ref[b].astype(F32)
        o = gb[:, :D_SSM] * jax.nn.sigmoid(gb[:, D_SSM:]) * (z * jax.nn.sigmoid(z))
        o_ref[b] = o.astype(o_ref.dtype)


def _ssm(proj, bw, cw, a_re_v, a_im_v, d_skip, w_glu):
    bsz, seq, _ = proj.shape
    tb = SSM_TB
    const = lambda shape: pl.BlockSpec(shape, lambda i: (0,) * len(shape),
                                       pipeline_mode=pl.Buffered(1))
    u_col = 4 * D_ATT // D_SSM
    n_state_vregs = 2 * bsz * (N_PAIRS // SUBLANES)
    return pl.pallas_call(
        _ssm_kernel,
        grid=(seq // tb,),
        in_specs=[pl.BlockSpec((bsz, tb, D_SSM), lambda i: (0, i, u_col)),
                  pl.BlockSpec((bsz, tb, D_SSM), lambda i: (0, i, u_col + 1)),
                  const((N_PAIRS, MXU_DIM, MXU_DIM)),
                  const((N_PAIRS, MXU_DIM, MXU_DIM)),
                  const((N_PAIRS // SUBLANES, SUBLANES, LANES)),
                  const((N_PAIRS // SUBLANES, SUBLANES, LANES)),
                  const((1, D_SSM)),
                  const((D_SSM, 2 * D_SSM))],
        out_specs=pl.BlockSpec((bsz, tb, D_SSM), lambda i: (0, i, 0)),
        out_shape=jax.ShapeDtypeStruct((bsz, seq, D_SSM), BF16),
        scratch_shapes=[pltpu.VMEM((bsz * N_PAIRS * SSM_PITCH, LANES), F32),
                        pltpu.VMEM((bsz * N_PAIRS * SSM_PITCH, LANES), F32),
                        pltpu.VMEM((n_state_vregs, SUBLANES, LANES), F32),
                        pltpu.VMEM((bsz * tb, D_SSM), BF16)],
        compiler_params=pltpu.CompilerParams(
            dimension_semantics=("arbitrary",),
            vmem_limit_bytes=VMEM_LIMIT),
        name="s5_glu",
    )(proj, proj, bw, cw, a_re_v, a_im_v, d_skip, w_glu)


def _ssm_weights(ab_re, ab_im, bb_re, bb_im, c_re, c_im):
    ppc = MXU_DIM // (2 * SSM_GROUP)
    sel = jax.nn.one_hot(jnp.arange(N_PAIRS) % ppc, ppc, dtype=F32)
    eye = jnp.eye(2, dtype=F32)
    shape5 = (2, N_PAIRS, 2, SSM_GROUP, STATE)
    bb = jnp.stack([bb_re, bb_im]).reshape(shape5)
    bw = jnp.einsum("rgkcp,gj,kl->gjkcrlp", bb, sel, eye)
    bw = bw.reshape(N_PAIRS, MXU_DIM, MXU_DIM).astype(BF16)
    cc = jnp.stack([c_re, -c_im]).reshape(shape5)
    cw = jnp.einsum("rgkcp,gj,kl->grkpjlc", cc, sel, eye)
    cw = cw.reshape(N_PAIRS, MXU_DIM, MXU_DIM).astype(BF16)
    vshape = (N_PAIRS // SUBLANES, SUBLANES, LANES)
    return bw, cw, ab_re.reshape(vshape), ab_im.reshape(vshape)


def _out_proj_kernel(oa_ref, os_ref, w_ref, g_ref, x_ref, o_ref):
    mix = (jnp.dot(oa_ref[...], w_ref[:D_ATT, :], preferred_element_type=F32)
           + jnp.dot(os_ref[...], w_ref[D_ATT:, :], preferred_element_type=F32))
    y = mix * lax.rsqrt(jnp.mean(mix * mix, axis=-1, keepdims=True) + EPS)
    o_ref[...] = x_ref[...] + y * g_ref[...]


def _out_proj(o_att, o_ssm, w, g, x2d):
    m = x2d.shape[0]
    return pl.pallas_call(
        _out_proj_kernel,
        grid=(m // OUT_TM,),
        in_specs=[pl.BlockSpec((OUT_TM, D_ATT), lambda i: (i, 0)),
                  pl.BlockSpec((OUT_TM, D_SSM), lambda i: (i, 0)),
                  pl.BlockSpec((D_MODEL, D_MODEL), lambda i: (0, 0),
                               pipeline_mode=pl.Buffered(1)),
                  pl.BlockSpec((1, D_MODEL), lambda i: (0, 0)),
                  pl.BlockSpec((OUT_TM, D_MODEL), lambda i: (i, 0))],
        out_specs=pl.BlockSpec((OUT_TM, D_MODEL), lambda i: (i, 0)),
        out_shape=jax.ShapeDtypeStruct((m, D_MODEL), F32),
        compiler_params=pltpu.CompilerParams(
            dimension_semantics=("arbitrary",),
            vmem_limit_bytes=VMEM_LIMIT),
        name="out_proj",
    )(o_att, o_ssm, w, g, x2d)


def kernel(x, rel_bias, pre_norm_g, post_norm_g, w_in, lambda_q1, lambda_k1, lambda_q2,
           lambda_k2, subln_g, ssm_a_re, ssm_a_im, ssm_log_dt, ssm_b_re, ssm_b_im,
           ssm_c_re, ssm_c_im, ssm_d, w_glu, w_out):
    bsz, seq, _ = x.shape
    m = bsz * seq
    ab_re, ab_im, bb_re, bb_im = _discretize(ssm_a_re, ssm_a_im, ssm_log_dt, ssm_b_re, ssm_b_im)
    col_scale = jnp.concatenate([jnp.full((D_ATT,), HEAD_DK ** -0.5 * LOG2E, F32),
                                 jnp.ones((D_IN - D_ATT,), F32)])
    x2d = x.reshape(m, D_MODEL)
    for l in range(DEPTH):
        lam_init = _lambda_init(l)
        w_in_l = (w_in[l] * col_scale).astype(BF16)
        proj = _in_proj(x2d, pre_norm_g[l].reshape(1, D_MODEL), w_in_l)
        proj = proj.reshape(bsz, seq, D_IN)
        scal = jnp.array([lam_init], F32)
        o_att = _attention(proj, scal, rel_bias,
                           lambda_q1[l].reshape(1, HEAD_DK), lambda_k1[l].reshape(1, HEAD_DK),
                           lambda_q2[l].reshape(1, HEAD_DK), lambda_k2[l].reshape(1, HEAD_DK),
                           subln_g[l].reshape(1, HEAD_DV))
        bw, cw, a_re_v, a_im_v = _ssm_weights(
            ab_re[l, :, 0, :], ab_im[l, :, 0, :], bb_re[l], bb_im[l], ssm_c_re[l], ssm_c_im[l])
        o_ssm = _ssm(proj, bw, cw, a_re_v, a_im_v, ssm_d[l].reshape(1, D_SSM),
                     w_glu[l].astype(BF16))
        x2d = _out_proj(o_att.reshape(m, D_ATT), o_ssm.reshape(m, D_SSM),
                        w_out[l].astype(BF16), post_norm_g[l].reshape(1, D_MODEL), x2d)
    return x2d.reshape(bsz, seq, D_MODEL)
```

```python
import math

import jax
import jax.numpy as jnp
from jax import lax
from jax.experimental import pallas as pl
from jax.experimental.pallas import tpu as pltpu

F32 = jnp.float32
BF16 = jnp.bfloat16

D_MODEL = 2048
DEPTH = 4
CHUNK = 64
D_ATT = 1024
D_SSM = 1024
N_HEADS = 8
HEAD_DV = 128
HEAD_DK = 64
SSM_GROUP = 16
N_GROUPS = 64
STATE = 64
N_BUCKETS = 32
MAX_DISTANCE = 128
EPS = 1e-6
NEG_INF = -1e30
LOG2E = math.log2(math.e)
D_IN = 4 * D_ATT + 2 * D_SSM

LANES = 128
SUBLANES = 8
MXU_DIM = 256
VMEM_LIMIT = 56 * 1024 * 1024

PROJ_TM = 1024
PROJ_TN = 1024
ATT_T = 512
ATT_HEADS = 2
SSM_TB = 256
SSM_PITCH = SSM_TB + SUBLANES
N_PAIRS = N_GROUPS // 2
OUT_TM = 512


def _lambda_init(layer_idx):
    return 0.8 - 0.6 * math.exp(-0.3 * layer_idx)


def _discretize_kernel(a_re_ref, a_im_ref, log_dt_ref, b_re_ref, b_im_ref,
                       ab_re_ref, ab_im_ref, bb_re_ref, bb_im_ref):
    a_re = a_re_ref[...]
    a_im = a_im_ref[...]
    dt = jnp.exp(log_dt_ref[...])
    mag = jnp.exp(dt * a_re)
    ab_re = mag * jnp.cos(dt * a_im)
    ab_im = mag * jnp.sin(dt * a_im)
    den = a_re * a_re + a_im * a_im
    nr = ab_re - 1.0
    cf_re = (nr * a_re + ab_im * a_im) / den
    cf_im = (ab_im * a_re - nr * a_im) / den
    b_re = b_re_ref[...]
    b_im = b_im_ref[...]
    ab_re_ref[...] = ab_re
    ab_im_ref[...] = ab_im
    bb_re_ref[...] = cf_re * b_re - cf_im * b_im
    bb_im_ref[...] = cf_re * b_im + cf_im * b_re


def _discretize(a_re, a_im, log_dt, b_re, b_im):
    nl = a_re.shape[0]
    a4 = (nl, N_GROUPS, 1, STATE)
    b4 = (nl, N_GROUPS, SSM_GROUP, STATE)
    return pl.pallas_call(
        _discretize_kernel,
        out_shape=(jax.ShapeDtypeStruct(a4, F32), jax.ShapeDtypeStruct(a4, F32),
                   jax.ShapeDtypeStruct(b4, F32), jax.ShapeDtypeStruct(b4, F32)),
        name="s5_discretize",
    )(a_re.reshape(a4), a_im.reshape(a4), log_dt.reshape(nl, N_GROUPS, 1, 1),
      jnp.swapaxes(b_re, 2, 3), jnp.swapaxes(b_im, 2, 3))


def _in_proj_kernel(x_ref, g_ref, w_ref, o_ref, h_ref):
    @pl.when(pl.program_id(1) == 0)
    def _():
        x = x_ref[...]
        y = x * lax.rsqrt(jnp.mean(x * x, axis=-1, keepdims=True) + EPS)
        h_ref[...] = (y * g_ref[...]).astype(BF16)

    o_ref[...] = jnp.dot(h_ref[...], w_ref[...],
                         preferred_element_type=F32).astype(o_ref.dtype)


def _in_proj(x2d, g, w):
    m = x2d.shape[0]
    return pl.pallas_call(
        _in_proj_kernel,
        grid=(m // PROJ_TM, D_IN // PROJ_TN),
        in_specs=[pl.BlockSpec((PROJ_TM, D_MODEL), lambda i, j: (i, 0)),
                  pl.BlockSpec((1, D_MODEL), lambda i, j: (0, 0)),
                  pl.BlockSpec((D_MODEL, PROJ_TN), lambda i, j: (0, j))],
        out_specs=pl.BlockSpec((PROJ_TM, PROJ_TN), lambda i, j: (i, j)),
        out_shape=jax.ShapeDtypeStruct((m, D_IN), BF16),
        scratch_shapes=[pltpu.VMEM((PROJ_TM, D_MODEL), BF16)],
        compiler_params=pltpu.CompilerParams(
            dimension_semantics=("arbitrary", "arbitrary"),
            vmem_limit_bytes=VMEM_LIMIT),
        name="in_proj",
    )(x2d, g, w)


def _attn_kernel(scal_ref, relb_ref, lq1_ref, lk1_ref, lq2_ref, lk2_ref, sg_ref,
                 q_ref, k_ref, v_ref, z_ref, o_ref,
                 bias_ref, m_ref, acc_ref, vx_ref, qm_ref, sa_ref, sb_ref, mxa_ref, mxb_ref):
    t = ATT_T
    n_lane_tiles = t // LANES
    h = pl.program_id(0)
    b = pl.program_id(1)
    qi = pl.program_id(2)

    @pl.when((b == 0) & (qi == 0))
    def _():
        w_rows, w_cols = LANES, 3 * LANES
        row = lax.broadcasted_iota(jnp.int32, (w_rows, w_cols), 0)
        col = lax.broadcasted_iota(jnp.int32, (w_rows, w_cols), 1) - LANES
        rel = col - row
        half = N_BUCKETS // 2
        max_exact = half // 2
        n = jnp.abs(rel)
        nf = jnp.maximum(n, 1).astype(F32)
        large = max_exact + (jnp.log(nf / max_exact) / math.log(MAX_DISTANCE / max_exact)
                             * (half - max_exact)).astype(jnp.int32)
        large = jnp.minimum(large, half - 1)
        bucket = jnp.where(rel > 0, half, 0) + jnp.where(n < max_exact, n, large)
        bias = jnp.zeros((w_rows, w_cols), F32)
        for i in range(N_BUCKETS):
            bias = jnp.where(bucket == i, relb_ref[i, h], bias)
        far = relb_ref[half - 1, h]
        allowed = ((col + LANES) // CHUNK - LANES // CHUNK) <= (row // CHUNK)
        window = jnp.where(allowed, (bias - far) * LOG2E, NEG_INF)
        for i in range(n_lane_tiles):
            rows = slice(i * LANES, (i + 1) * LANES)
            lo = (n_lane_tiles + i - 1) * LANES
            hi = min(lo + w_cols, 2 * t)
            bias_ref[rows, :lo] = jnp.zeros((LANES, lo), F32)
            bias_ref[rows, lo:hi] = window[:, :hi - lo]
            if hi < 2 * t:
                bias_ref[rows, hi:] = jnp.full((LANES, 2 * t - hi), NEG_INF, F32)

    @pl.when(qi == 0)
    def _():
        vx_ref[:, :HEAD_DV] = v_ref[...]
        vx_ref[:, HEAD_DV:] = jnp.ones((v_ref.shape[0], HEAD_DV), BF16)

    q = q_ref[...]
    lane = lax.broadcasted_iota(jnp.int32, (t, HEAD_DV), 1)
    zero = jnp.zeros_like(q)
    qm_ref[0] = jnp.where(lane < HEAD_DK, q, zero)
    qm_ref[1] = jnp.where(lane >= HEAD_DK, q, zero)

    def qk_scores(s_ref, mx_ref, tile, bias=None):
        kj = k_ref[pl.ds(pl.multiple_of(tile * t, t), t), :]
        if bias is None:
            corner = jnp.where(tile == qi - 1, bias_ref[:LANES, t - LANES:t], 0.0)
        for mi in range(2):
            s = lax.dot_general(qm_ref[mi], kj, (((1,), (1,)), ((), ())),
                                preferred_element_type=F32)
            if bias is None:
                top = jnp.concatenate([s[:LANES, :t - LANES], s[:LANES, t - LANES:] + corner],
                                      axis=1)
                s = jnp.concatenate([top, s[LANES:]], axis=0)
            else:
                s = s + bias
            s_ref[mi] = s
            mx_ref[mi] = jnp.broadcast_to(jnp.max(s, axis=1, keepdims=True), (t, LANES))

    def first_tile(s_ref, mx_ref, tile):
        vj = vx_ref[pl.ds(pl.multiple_of(tile * t, t), t), :]
        for mi in range(2):
            m_new = mx_ref[mi]
            p = jnp.exp2(s_ref[mi] - jnp.concatenate([m_new] * n_lane_tiles, axis=1))
            m_ref[mi] = m_new
            acc_ref[mi] = jnp.dot(p.astype(BF16), vj, preferred_element_type=F32)

    def next_tile(s_ref, mx_ref, tile):
        vj = vx_ref[pl.ds(pl.multiple_of(tile * t, t), t), :]
        for mi in range(2):
            m_prev = m_ref[mi]
            m_new = jnp.maximum(m_prev, mx_ref[mi])
            alpha = jnp.exp2(m_prev - m_new)
            p = jnp.exp2(s_ref[mi] - jnp.concatenate([m_new] * n_lane_tiles, axis=1))
            m_ref[mi] = m_new
            acc_ref[mi] = (jnp.concatenate([alpha, alpha], axis=1) * acc_ref[mi]
                           + jnp.dot(p.astype(BF16), vj, preferred_element_type=F32))

    n_prev = qi
    last = jnp.maximum(n_prev - 1, 0)
    qk_scores(sa_ref, mxa_ref, qi, bias_ref[:, t:])
    qk_scores(sb_ref, mxb_ref, 0)
    first_tile(sa_ref, mxa_ref, qi)

    def pair_body(i, carry):
        qk_scores(sa_ref, mxa_ref, jnp.minimum(2 * i + 1, last))
        next_tile(sb_ref, mxb_ref, 2 * i)
        qk_scores(sb_ref, mxb_ref, jnp.minimum(2 * i + 2, last))
        next_tile(sa_ref, mxa_ref, 2 * i + 1)
        return carry

    lax.fori_loop(0, n_prev // 2, pair_body, 0)

    @pl.when(n_prev % 2 == 1)
    def _():
        next_tile(sb_ref, mxb_ref, last)

    lam_init = scal_ref[0]
    lam = (jnp.exp(jnp.sum(lq1_ref[...] * lk1_ref[...], axis=1, keepdims=True))
           - jnp.exp(jnp.sum(lq2_ref[...] * lk2_ref[...], axis=1, keepdims=True))
           + lam_init)
    o = (acc_ref[0, :, :HEAD_DV] / acc_ref[0, :, HEAD_DV:]
         - lam * (acc_ref[1, :, :HEAD_DV] / acc_ref[1, :, HEAD_DV:]))
    o = o * lax.rsqrt(jnp.mean(o * o, axis=-1, keepdims=True) + EPS) * sg_ref[...]
    o = o * (1.0 - lam_init)
    z = z_ref[...].astype(F32)
    o_ref[...] = (o * (z * jax.nn.sigmoid(z))).astype(o_ref.dtype)


def _attention(proj, scal, rel_bias, lq1, lk1, lq2, lk2, subln_g):
    bsz, seq, _ = proj.shape
    t = ATT_T
    smem = pl.BlockSpec(memory_space=pltpu.SMEM)
    vec = lambda n: pl.BlockSpec((1, n), lambda h, b, i: (0, 0))
    return pl.pallas_call(
        _attn_kernel,
        grid=(N_HEADS, bsz, seq // t),
        in_specs=[smem, smem, vec(HEAD_DK), vec(HEAD_DK), vec(HEAD_DK), vec(HEAD_DK),
                  vec(HEAD_DV),
                  pl.BlockSpec((None, t, HEAD_DV), lambda h, b, i: (b, i, h)),
                  pl.BlockSpec((None, seq, HEAD_DV), lambda h, b, i: (b, 0, N_HEADS + h)),
                  pl.BlockSpec((None, seq, HEAD_DV), lambda h, b, i: (b, 0, 2 * N_HEADS + h)),
                  pl.BlockSpec((None, t, HEAD_DV), lambda h, b, i: (b, i, 3 * N_HEADS + h))],
        out_specs=pl.BlockSpec((None, t, HEAD_DV), lambda h, b, i: (b, i, h)),
        out_shape=jax.ShapeDtypeStruct((bsz, seq, D_ATT), BF16),
        scratch_shapes=[pltpu.VMEM((t, 2 * t), F32),
                        pltpu.VMEM((2, t, LANES), F32),
                        pltpu.VMEM((2, t, 2 * HEAD_DV), F32),
                        pltpu.VMEM((seq, 2 * HEAD_DV), BF16),
                        pltpu.VMEM((2, t, HEAD_DV), BF16),
                        pltpu.VMEM((2, t, t), F32),
                        pltpu.VMEM((2, t, t), F32),
                        pltpu.VMEM((2, t, LANES), F32),
                        pltpu.VMEM((2, t, LANES), F32)],
        compiler_params=pltpu.CompilerParams(
            dimension_semantics=("arbitrary", "arbitrary", "arbitrary"),
            vmem_limit_bytes=VMEM_LIMIT),
        name="diff_attention",
    )(scal, rel_bias, lq1, lk1, lq2, lk2, subln_g, proj, proj, proj, proj)


def _attn_kernel_mh(scal_ref, relb_ref, lq1_ref, lk1_ref, lq2_ref, lk2_ref, sg_ref,
                    q_ref, k_ref, v_ref, z_ref, o_ref,
                    bias_ref, m_ref, acc_ref, vx_ref, qm_ref, sa_ref, sb_ref, mxa_ref, mxb_ref):
    t = ATT_T
    n_lane_tiles = t // LANES
    heads = range(ATT_HEADS)
    hp = pl.program_id(0)
    b = pl.program_id(1)
    qi = pl.program_id(2)
    lanes_of = lambda hh: slice(hh * HEAD_DV, (hh + 1) * HEAD_DV)

    @pl.when((b == 0) & (qi == 0))
    def _():
        w_rows, w_cols = LANES, 3 * LANES
        row = lax.broadcasted_iota(jnp.int32, (w_rows, w_cols), 0)
        col = lax.broadcasted_iota(jnp.int32, (w_rows, w_cols), 1) - LANES
        rel = col - row
        half = N_BUCKETS // 2
        max_exact = half // 2
        n = jnp.abs(rel)
        nf = jnp.maximum(n, 1).astype(F32)
        large = max_exact + (jnp.log(nf / max_exact) / math.log(MAX_DISTANCE / max_exact)
                             * (half - max_exact)).astype(jnp.int32)
        large = jnp.minimum(large, half - 1)
        bucket = jnp.where(rel > 0, half, 0) + jnp.where(n < max_exact, n, large)
        allowed = ((col + LANES) // CHUNK - LANES // CHUNK) <= (row // CHUNK)
        for hh in heads:
            h = hp * ATT_HEADS + hh
            bias = jnp.zeros((w_rows, w_cols), F32)
            for i in range(N_BUCKETS):
                bias = jnp.where(bucket == i, relb_ref[i, h], bias)
            far = relb_ref[half - 1, h]
            window = jnp.where(allowed, (bias - far) * LOG2E, NEG_INF)
            for i in range(n_lane_tiles):
                rows = slice(i * LANES, (i + 1) * LANES)
                lo = (n_lane_tiles + i - 1) * LANES
                hi = min(lo + w_cols, 2 * t)
                bias_ref[hh, rows, :lo] = jnp.zeros((LANES, lo), F32)
                bias_ref[hh, rows, lo:hi] = window[:, :hi - lo]
                if hi < 2 * t:
                    bias_ref[hh, rows, hi:] = jnp.full((LANES, 2 * t - hi), NEG_INF, F32)

    @pl.when(qi == 0)
    def _():
        for hh in heads:
            vx_ref[hh, :, :HEAD_DV] = v_ref[:, lanes_of(hh)]
            vx_ref[hh, :, HEAD_DV:] = jnp.ones((v_ref.shape[0], HEAD_DV), BF16)

    lane = lax.broadcasted_iota(jnp.int32, (t, HEAD_DV), 1)
    for hh in heads:
        q = q_ref[:, lanes_of(hh)]
        zero = jnp.zeros_like(q)
        qm_ref[hh, 0] = jnp.where(lane < HEAD_DK, q, zero)
        qm_ref[hh, 1] = jnp.where(lane >= HEAD_DK, q, zero)

    def qk_scores(hh, s_ref, mx_ref, tile, diagonal=False):
        kj = k_ref[pl.ds(pl.multiple_of(tile * t, t), t), lanes_of(hh)]
        if diagonal:
            bias = bias_ref[hh, :, t:]
        else:
            corner = jnp.where(tile == qi - 1, bias_ref[hh, :LANES, t - LANES:t], 0.0)
        for mi in range(2):
            s = lax.dot_general(qm_ref[hh, mi], kj, (((1,), (1,)), ((), ())),
                                preferred_element_type=F32)
            if diagonal:
                s = s + bias
            else:
                top = jnp.concatenate([s[:LANES, :t - LANES], s[:LANES, t - LANES:] + corner],
                                      axis=1)
                s = jnp.concatenate([top, s[LANES:]], axis=0)
            s_ref[hh, mi] = s
            mx_ref[hh, mi] = jnp.broadcast_to(jnp.max(s, axis=1, keepdims=True), (t, LANES))

    def first_tile(hh, s_ref, mx_ref, tile):
        vj = vx_ref[hh, pl.ds(pl.multiple_of(tile * t, t), t), :]
        for mi in range(2):
            m_new = mx_ref[hh, mi]
            p = jnp.exp2(s_ref[hh, mi] - jnp.concatenate([m_new] * n_lane_tiles, axis=1))
            m_ref[hh, mi] = m_new
            acc_ref[hh, mi] = jnp.dot(p.astype(BF16), vj, preferred_element_type=F32)

    def next_tile(hh, s_ref, mx_ref, tile):
        vj = vx_ref[hh, pl.ds(pl.multiple_of(tile * t, t), t), :]
        for mi in range(2):
            m_prev = m_ref[hh, mi]
            m_new = jnp.maximum(m_prev, mx_ref[hh, mi])
            alpha = jnp.exp2(m_prev - m_new)
            p = jnp.exp2(s_ref[hh, mi] - jnp.concatenate([m_new] * n_lane_tiles, axis=1))
            m_ref[hh, mi] = m_new
            acc_ref[hh, mi] = (jnp.concatenate([alpha, alpha], axis=1) * acc_ref[hh, mi]
                               + jnp.dot(p.astype(BF16), vj, preferred_element_type=F32))

    n_prev = qi
    last = jnp.maximum(n_prev - 1, 0)
    for hh in heads:
        qk_scores(hh, sa_ref, mxa_ref, qi, diagonal=True)
    for hh in heads:
        qk_scores(hh, sb_ref, mxb_ref, 0)
    for hh in heads:
        first_tile(hh, sa_ref, mxa_ref, qi)

    def pair_body(i, carry):
        for hh in heads:
            qk_scores(hh, sa_ref, mxa_ref, jnp.minimum(2 * i + 1, last))
        for hh in heads:
            next_tile(hh, sb_ref, mxb_ref, 2 * i)
        for hh in heads:
            qk_scores(hh, sb_ref, mxb_ref, jnp.minimum(2 * i + 2, last))
        for hh in heads:
            next_tile(hh, sa_ref, mxa_ref, 2 * i + 1)
        return carry

    lax.fori_loop(0, n_prev // 2, pair_body, 0)

    @pl.when(n_prev % 2 == 1)
    def _():
        for hh in heads:
            next_tile(hh, sb_ref, mxb_ref, last)

    lam_init = scal_ref[0]
    lam = (jnp.exp(jnp.sum(lq1_ref[...] * lk1_ref[...], axis=1, keepdims=True))
           - jnp.exp(jnp.sum(lq2_ref[...] * lk2_ref[...], axis=1, keepdims=True))
           + lam_init)
    for hh in heads:
        o = (acc_ref[hh, 0, :, :HEAD_DV] / acc_ref[hh, 0, :, HEAD_DV:]
             - lam * (acc_ref[hh, 1, :, :HEAD_DV] / acc_ref[hh, 1, :, HEAD_DV:]))
        o = o * lax.rsqrt(jnp.mean(o * o, axis=-1, keepdims=True) + EPS) * sg_ref[...]
        o = o * (1.0 - lam_init)
        z = z_ref[:, lanes_of(hh)].astype(F32)
        o_ref[:, lanes_of(hh)] = (o * (z * jax.nn.sigmoid(z))).astype(o_ref.dtype)


def _attention_mh(proj, scal, rel_bias, lq1, lk1, lq2, lk2, subln_g):
    bsz, seq, _ = proj.shape
    t = ATT_T
    nh = ATT_HEADS
    width = nh * HEAD_DV
    groups = N_HEADS // nh
    smem = pl.BlockSpec(memory_space=pltpu.SMEM)
    vec = lambda n: pl.BlockSpec((1, n), lambda g, b, i: (0, 0))
    return pl.pallas_call(
        _attn_kernel_mh,
        grid=(groups, bsz, seq // t),
        in_specs=[smem, smem, vec(HEAD_DK), vec(HEAD_DK), vec(HEAD_DK), vec(HEAD_DK),
                  vec(HEAD_DV),
                  pl.BlockSpec((None, t, width), lambda g, b, i: (b, i, g)),
                  pl.BlockSpec((None, seq, width), lambda g, b, i: (b, 0, groups + g)),
                  pl.BlockSpec((None, seq, width), lambda g, b, i: (b, 0, 2 * groups + g)),
                  pl.BlockSpec((None, t, width), lambda g, b, i: (b, i, 3 * groups + g))],
        out_specs=pl.BlockSpec((None, t, width), lambda g, b, i: (b, i, g)),
        out_shape=jax.ShapeDtypeStruct((bsz, seq, D_ATT), BF16),
        scratch_shapes=[pltpu.VMEM((nh, t, 2 * t), F32),
                        pltpu.VMEM((nh, 2, t, LANES), F32),
                        pltpu.VMEM((nh, 2, t, 2 * HEAD_DV), F32),
                        pltpu.VMEM((nh, seq, 2 * HEAD_DV), BF16),
                        pltpu.VMEM((nh, 2, t, HEAD_DV), BF16),
                        pltpu.VMEM((nh, 2, t, t), F32),
                        pltpu.VMEM((nh, 2, t, t), F32),
                        pltpu.VMEM((nh, 2, t, LANES), F32),
                        pltpu.VMEM((nh, 2, t, LANES), F32)],
        compiler_params=pltpu.CompilerParams(
            dimension_semantics=("arbitrary", "arbitrary", "arbitrary"),
            vmem_limit_bytes=VMEM_LIMIT),
        name="diff_attention",
    )(scal, rel_bias, lq1, lk1, lq2, lk2, subln_g, proj, proj, proj, proj)


def _gelu_tanh(x):
    return 0.5 * x * (1.0 + jnp.tanh(math.sqrt(2.0 / math.pi) * (x + 0.044715 * (x * x * x))))


def _ssm_kernel(u_ref, z_ref, bw_ref, cw_ref, are_ref, aim_ref, d_ref, wglu_ref,
                o_ref, hre_ref, him_ref, st_ref, g_ref):
    nb = u_ref.shape[0]
    tb = SSM_TB
    pitch = SSM_PITCH
    pairs_per_chunk = MXU_DIM // (2 * SSM_GROUP)
    n_vregs = N_PAIRS // SUBLANES

    def pair_rows(b, gp):
        return slice((b * N_PAIRS + gp) * pitch, (b * N_PAIRS + gp) * pitch + tb)

    @pl.when(pl.program_id(0) == 0)
    def _():
        st_ref[...] = jnp.zeros_like(st_ref)

    for gp in range(N_PAIRS):
        kc = gp // pairs_per_chunk
        uc = jnp.concatenate([u_ref[b, :, kc * MXU_DIM:(kc + 1) * MXU_DIM] for b in range(nb)],
                             axis=0)
        bu = jnp.dot(uc, bw_ref[gp], preferred_element_type=F32)
        for b in range(nb):
            hre_ref[pair_rows(b, gp), :] = bu[b * tb:(b + 1) * tb, :LANES]
            him_ref[pair_rows(b, gp), :] = bu[b * tb:(b + 1) * tb, LANES:]

    a_re = [are_ref[k] for k in range(n_vregs)]
    a_im = [aim_ref[k] for k in range(n_vregs)]

    def step(ti, carry):
        out = []
        for b in range(nb):
            for k in range(n_vregs):
                h_re, h_im = carry[2 * (b * n_vregs + k)], carry[2 * (b * n_vregs + k) + 1]
                rows = pl.ds((b * N_PAIRS + k * SUBLANES) * pitch + ti, SUBLANES, stride=pitch)
                n_re = a_re[k] * h_re - a_im[k] * h_im + hre_ref[rows, :]
                n_im = a_re[k] * h_im + a_im[k] * h_re + him_ref[rows, :]
                hre_ref[rows, :] = n_re
                him_ref[rows, :] = n_im
                out += [n_re, n_im]
        return tuple(out)

    init = tuple(st_ref[i] for i in range(2 * nb * n_vregs))
    final = lax.fori_loop(0, tb, step, init, unroll=2)
    for i, s in enumerate(final):
        st_ref[i] = s

    for kc in range(D_SSM // MXU_DIM):
        gps = range(kc * pairs_per_chunk, (kc + 1) * pairs_per_chunk)
        hcat = jnp.concatenate(
            [jnp.concatenate([ref[pair_rows(b, gp), :]
                              for gp in gps for ref in (hre_ref, him_ref)], axis=1)
             for b in range(nb)], axis=0).astype(BF16)
        cwk = cw_ref[kc * pairs_per_chunk:(kc + 1) * pairs_per_chunk]
        acc = jnp.dot(hcat, cwk.reshape(pairs_per_chunk * MXU_DIM, MXU_DIM),
                      preferred_element_type=F32)
        cols = slice(kc * MXU_DIM, (kc + 1) * MXU_DIM)
        for b in range(nb):
            y = acc[b * tb:(b + 1) * tb] + d_ref[:, cols] * u_ref[b, :, cols].astype(F32)
            g_ref[b * tb:(b + 1) * tb, cols] = _gelu_tanh(y).astype(BF16)

    glu = jnp.dot(g_ref[...], wglu_ref[...], preferred_element_type=F32)
    for b in range(nb):
        gb = glu[b * tb:(b + 1) * tb]
        z = z_ref[b].astype(F32)
        o = gb[:, :D_SSM] * jax.nn.sigmoid(gb[:, D_SSM:]) * (z * jax.nn.sigmoid(z))
        o_ref[b] = o.astype(o_ref.dtype)


def _ssm(proj, bw, cw, a_re_v, a_im_v, d_skip, w_glu):
    bsz, seq, _ = proj.shape
    tb = SSM_TB
    const = lambda shape: pl.BlockSpec(shape, lambda i: (0,) * len(shape),
                                       pipeline_mode=pl.Buffered(1))
    u_col = 4 * D_ATT // D_SSM
    n_state_vregs = 2 * bsz * (N_PAIRS // SUBLANES)
    return pl.pallas_call(
        _ssm_kernel,
        grid=(seq // tb,),
        in_specs=[pl.BlockSpec((bsz, tb, D_SSM), lambda i: (0, i, u_col)),
                  pl.BlockSpec((bsz, tb, D_SSM), lambda i: (0, i, u_col + 1)),
                  const((N_PAIRS, MXU_DIM, MXU_DIM)),
                  const((N_PAIRS, MXU_DIM, MXU_DIM)),
                  const((N_PAIRS // SUBLANES, SUBLANES, LANES)),
                  const((N_PAIRS // SUBLANES, SUBLANES, LANES)),
                  const((1, D_SSM)),
                  const((D_SSM, 2 * D_SSM))],
        out_specs=pl.BlockSpec((bsz, tb, D_SSM), lambda i: (0, i, 0)),
        out_shape=jax.ShapeDtypeStruct((bsz, seq, D_SSM), BF16),
        scratch_shapes=[pltpu.VMEM((bsz * N_PAIRS * SSM_PITCH, LANES), F32),
                        pltpu.VMEM((bsz * N_PAIRS * SSM_PITCH, LANES), F32),
                        pltpu.VMEM((n_state_vregs, SUBLANES, LANES), F32),
                        pltpu.VMEM((bsz * tb, D_SSM), BF16)],
        compiler_params=pltpu.CompilerParams(
            dimension_semantics=("arbitrary",),
            vmem_limit_bytes=VMEM_LIMIT),
        name="s5_glu",
    )(proj, proj, bw, cw, a_re_v, a_im_v, d_skip, w_glu)


def _ssm_weights(ab_re, ab_im, bb_re, bb_im, c_re, c_im):
    ppc = MXU_DIM // (2 * SSM_GROUP)
    sel = (jnp.arange(N_PAIRS)[:, None] % ppc == jnp.arange(ppc)[None, :]).astype(F32)
    eye = jnp.eye(2, dtype=F32)
    shape5 = (2, N_PAIRS, 2, SSM_GROUP, STATE)
    bb = jnp.stack([bb_re, bb_im]).reshape(shape5)
    bb = jnp.transpose(bb, (1, 2, 3, 0, 4))
    bw = (bb[:, None, :, :, :, None, :] * sel[:, :, None, None, None, None, None]
          * eye[None, None, :, None, None, :, None])
    bw = bw.astype(BF16).reshape(N_PAIRS, MXU_DIM, MXU_DIM)
    cc = jnp.stack([c_re, -c_im]).reshape(shape5)
    cc = jnp.transpose(cc, (1, 0, 2, 4, 3))
    cw = (cc[:, :, :, :, None, None, :] * sel[:, None, None, None, :, None, None]
          * eye[None, None, :, None, None, :, None])
    cw = cw.astype(BF16).reshape(N_PAIRS, MXU_DIM, MXU_DIM)
    vshape = (N_PAIRS // SUBLANES, SUBLANES, LANES)
    return bw, cw, ab_re.reshape(vshape), ab_im.reshape(vshape)


def _out_proj_kernel(oa_ref, os_ref, w_ref, g_ref, x_ref, o_ref):
    mix = (jnp.dot(oa_ref[...], w_ref[:D_ATT, :], preferred_element_type=F32)
           + jnp.dot(os_ref[...], w_ref[D_ATT:, :], preferred_element_type=F32))
    y = mix * lax.rsqrt(jnp.mean(mix * mix, axis=-1, keepdims=True) + EPS)
    o_ref[...] = x_ref[...] + y * g_ref[...]


def _out_proj(o_att, o_ssm, w, g, x2d):
    m = x2d.shape[0]
    return pl.pallas_call(
        _out_proj_kernel,
        grid=(m // OUT_TM,),
        in_specs=[pl.BlockSpec((OUT_TM, D_ATT), lambda i: (i, 0)),
                  pl.BlockSpec((OUT_TM, D_SSM), lambda i: (i, 0)),
                  pl.BlockSpec((D_MODEL, D_MODEL), lambda i: (0, 0),
                               pipeline_mode=pl.Buffered(1)),
                  pl.BlockSpec((1, D_MODEL), lambda i: (0, 0)),
                  pl.BlockSpec((OUT_TM, D_MODEL), lambda i: (i, 0))],
        out_specs=pl.BlockSpec((OUT_TM, D_MODEL), lambda i: (i, 0)),
        out_shape=jax.ShapeDtypeStruct((m, D_MODEL), F32),
        compiler_params=pltpu.CompilerParams(
            dimension_semantics=("arbitrary",),
            vmem_limit_bytes=VMEM_LIMIT),
        name="out_proj",
    )(o_att, o_ssm, w, g, x2d)


def kernel(x, rel_bias, pre_norm_g, post_norm_g, w_in, lambda_q1, lambda_k1, lambda_q2,
           lambda_k2, subln_g, ssm_a_re, ssm_a_im, ssm_log_dt, ssm_b_re, ssm_b_im,
           ssm_c_re, ssm_c_im, ssm_d, w_glu, w_out):
    bsz, seq, _ = x.shape
    m = bsz * seq
    ab_re, ab_im, bb_re, bb_im = _discretize(ssm_a_re, ssm_a_im, ssm_log_dt, ssm_b_re, ssm_b_im)
    col_scale = jnp.concatenate([jnp.full((D_ATT,), HEAD_DK ** -0.5 * LOG2E, F32),
                                 jnp.ones((D_IN - D_ATT,), F32)])
    x2d = x.reshape(m, D_MODEL)
    for l in range(DEPTH):
        lam_init = _lambda_init(l)
        w_in_l = (w_in[l] * col_scale).astype(BF16)
        proj = _in_proj(x2d, pre_norm_g[l].reshape(1, D_MODEL), w_in_l)
        proj = proj.reshape(bsz, seq, D_IN)
        scal = jnp.array([lam_init], F32)
        o_att = _attention(proj, scal, rel_bias,
                           lambda_q1[l].reshape(1, HEAD_DK), lambda_k1[l].reshape(1, HEAD_DK),
                           lambda_q2[l].reshape(1, HEAD_DK), lambda_k2[l].reshape(1, HEAD_DK),
                           subln_g[l].reshape(1, HEAD_DV))
        bw, cw, a_re_v, a_im_v = _ssm_weights(
            ab_re[l, :, 0, :], ab_im[l, :, 0, :], bb_re[l], bb_im[l], ssm_c_re[l], ssm_c_im[l])
        o_ssm = _ssm(proj, bw, cw, a_re_v, a_im_v, ssm_d[l].reshape(1, D_SSM),
                     w_glu[l].astype(BF16))
        x2d = _out_proj(o_att.reshape(m, D_ATT), o_ssm.reshape(m, D_SSM),
                        w_out[l].astype(BF16), post_norm_g[l].reshape(1, D_MODEL), x2d)
    return x2d.reshape(bsz, seq, D_MODEL)
```

```python
import math

import jax
import jax.numpy as jnp
from jax import lax
from jax.experimental import pallas as pl
from jax.experimental.pallas import tpu as pltpu

F32 = jnp.float32
BF16 = jnp.bfloat16

D_MODEL = 2048
DEPTH = 4
CHUNK = 64
D_ATT = 1024
D_SSM = 1024
N_HEADS = 8
HEAD_DV = 128
HEAD_DK = 64
SSM_GROUP = 16
N_GROUPS = 64
STATE = 64
N_BUCKETS = 32
MAX_DISTANCE = 128
EPS = 1e-6
NEG_INF = -1e30
LOG2E = math.log2(math.e)
D_IN = 4 * D_ATT + 2 * D_SSM

LANES = 128
SUBLANES = 8
MXU_DIM = 256
VMEM_LIMIT = 56 * 1024 * 1024

PROJ_TM = 1024
PROJ_TN = 1024
ATT_T = 512
ATT_HEADS = 2
SSM_TB = 256
SSM_PITCH = SSM_TB + SUBLANES // 2
N_PAIRS = N_GROUPS // 2
OUT_TM = 512


def _lambda_init(layer_idx):
    return 0.8 - 0.6 * math.exp(-0.3 * layer_idx)


def _discretize_kernel(a_re_ref, a_im_ref, log_dt_ref, b_re_ref, b_im_ref,
                       ab_re_ref, ab_im_ref, bb_re_ref, bb_im_ref):
    a_re = a_re_ref[...]
    a_im = a_im_ref[...]
    dt = jnp.exp(log_dt_ref[...])
    mag = jnp.exp(dt * a_re)
    ab_re = mag * jnp.cos(dt * a_im)
    ab_im = mag * jnp.sin(dt * a_im)
    den = a_re * a_re + a_im * a_im
    nr = ab_re - 1.0
    cf_re = (nr * a_re + ab_im * a_im) / den
    cf_im = (ab_im * a_re - nr * a_im) / den
    b_re = b_re_ref[...]
    b_im = b_im_ref[...]
    ab_re_ref[...] = ab_re
    ab_im_ref[...] = ab_im
    bb_re_ref[...] = cf_re * b_re - cf_im * b_im
    bb_im_ref[...] = cf_re * b_im + cf_im * b_re


def _discretize(a_re, a_im, log_dt, b_re, b_im):
    nl = a_re.shape[0]
    a4 = (nl, N_GROUPS, 1, STATE)
    b4 = (nl, N_GROUPS, SSM_GROUP, STATE)
    return pl.pallas_call(
        _discretize_kernel,
        out_shape=(jax.ShapeDtypeStruct(a4, F32), jax.ShapeDtypeStruct(a4, F32),
                   jax.ShapeDtypeStruct(b4, F32), jax.ShapeDtypeStruct(b4, F32)),
        name="s5_discretize",
    )(a_re.reshape(a4), a_im.reshape(a4), log_dt.reshape(nl, N_GROUPS, 1, 1),
      jnp.swapaxes(b_re, 2, 3), jnp.swapaxes(b_im, 2, 3))


def _in_proj_kernel(x_ref, g_ref, w_ref, o_ref, h_ref):
    @pl.when(pl.program_id(1) == 0)
    def _():
        x = x_ref[...]
        y = x * lax.rsqrt(jnp.mean(x * x, axis=-1, keepdims=True) + EPS)
        h_ref[...] = (y * g_ref[...]).astype(BF16)

    o_ref[...] = jnp.dot(h_ref[...], w_ref[...],
                         preferred_element_type=F32).astype(o_ref.dtype)


def _in_proj(x2d, g, w):
    m = x2d.shape[0]
    return pl.pallas_call(
        _in_proj_kernel,
        grid=(m // PROJ_TM, D_IN // PROJ_TN),
        in_specs=[pl.BlockSpec((PROJ_TM, D_MODEL), lambda i, j: (i, 0)),
                  pl.BlockSpec((1, D_MODEL), lambda i, j: (0, 0)),
                  pl.BlockSpec((D_MODEL, PROJ_TN), lambda i, j: (0, j))],
        out_specs=pl.BlockSpec((PROJ_TM, PROJ_TN), lambda i, j: (i, j)),
        out_shape=jax.ShapeDtypeStruct((m, D_IN), BF16),
        scratch_shapes=[pltpu.VMEM((PROJ_TM, D_MODEL), BF16)],
        compiler_params=pltpu.CompilerParams(
            dimension_semantics=("arbitrary", "arbitrary"),
            vmem_limit_bytes=VMEM_LIMIT),
        name="in_proj",
    )(x2d, g, w)


def _attn_kernel(scal_ref, relb_ref, lq1_ref, lk1_ref, lq2_ref, lk2_ref, sg_ref,
                 q_ref, k_ref, v_ref, z_ref, o_ref,
                 bias_ref, m_ref, acc_ref, vx_ref, qm_ref, sa_ref, sb_ref, mxa_ref, mxb_ref):
    t = ATT_T
    n_lane_tiles = t // LANES
    h = pl.program_id(0)
    b = pl.program_id(1)
    qi = pl.program_id(2)

    @pl.when((b == 0) & (qi == 0))
    def _():
        w_rows, w_cols = LANES, 3 * LANES
        row = lax.broadcasted_iota(jnp.int32, (w_rows, w_cols), 0)
        col = lax.broadcasted_iota(jnp.int32, (w_rows, w_cols), 1) - LANES
        rel = col - row
        half = N_BUCKETS // 2
        max_exact = half // 2
        n = jnp.abs(rel)
        nf = jnp.maximum(n, 1).astype(F32)
        large = max_exact + (jnp.log(nf / max_exact) / math.log(MAX_DISTANCE / max_exact)
                             * (half - max_exact)).astype(jnp.int32)
        large = jnp.minimum(large, half - 1)
        bucket = jnp.where(rel > 0, half, 0) + jnp.where(n < max_exact, n, large)
        bias = jnp.zeros((w_rows, w_cols), F32)
        for i in range(N_BUCKETS):
            bias = jnp.where(bucket == i, relb_ref[i, h], bias)
        far = relb_ref[half - 1, h]
        allowed = ((col + LANES) // CHUNK - LANES // CHUNK) <= (row // CHUNK)
        window = jnp.where(allowed, (bias - far) * LOG2E, NEG_INF)
        for i in range(n_lane_tiles):
            rows = slice(i * LANES, (i + 1) * LANES)
            lo = (n_lane_tiles + i - 1) * LANES
            hi = min(lo + w_cols, 2 * t)
            bias_ref[rows, :lo] = jnp.zeros((LANES, lo), F32)
            bias_ref[rows, lo:hi] = window[:, :hi - lo]
            if hi < 2 * t:
                bias_ref[rows, hi:] = jnp.full((LANES, 2 * t - hi), NEG_INF, F32)

    @pl.when(qi == 0)
    def _():
        vx_ref[:, :HEAD_DV] = v_ref[...]
        vx_ref[:, HEAD_DV:] = jnp.ones((v_ref.shape[0], HEAD_DV), BF16)

    q = q_ref[...]
    lane = lax.broadcasted_iota(jnp.int32, (t, HEAD_DV), 1)
    zero = jnp.zeros_like(q)
    qm_ref[0] = jnp.where(lane < HEAD_DK, q, zero)
    qm_ref[1] = jnp.where(lane >= HEAD_DK, q, zero)

    def qk_scores(s_ref, mx_ref, tile, bias=None):
        kj = k_ref[pl.ds(pl.multiple_of(tile * t, t), t), :]
        if bias is None:
            corner = jnp.where(tile == qi - 1, bias_ref[:LANES, t - LANES:t], 0.0)
        for mi in range(2):
            s = lax.dot_general(qm_ref[mi], kj, (((1,), (1,)), ((), ())),
                                preferred_element_type=F32)
            if bias is None:
                top = jnp.concatenate([s[:LANES, :t - LANES], s[:LANES, t - LANES:] + corner],
                                      axis=1)
                s = jnp.concatenate([top, s[LANES:]], axis=0)
            else:
                s = s + bias
            s_ref[mi] = s
            mx_ref[mi] = jnp.broadcast_to(jnp.max(s, axis=1, keepdims=True), (t, LANES))

    def first_tile(s_ref, mx_ref, tile):
        vj = vx_ref[pl.ds(pl.multiple_of(tile * t, t), t), :]
        for mi in range(2):
            m_new = mx_ref[mi]
            p = jnp.exp2(s_ref[mi] - jnp.concatenate([m_new] * n_lane_tiles, axis=1))
            m_ref[mi] = m_new
            acc_ref[mi] = jnp.dot(p.astype(BF16), vj, preferred_element_type=F32)

    def next_tile(s_ref, mx_ref, tile):
        vj = vx_ref[pl.ds(pl.multiple_of(tile * t, t), t), :]
        for mi in range(2):
            m_prev = m_ref[mi]
            m_new = jnp.maximum(m_prev, mx_ref[mi])
            alpha = jnp.exp2(m_prev - m_new)
            p = jnp.exp2(s_ref[mi] - jnp.concatenate([m_new] * n_lane_tiles, axis=1))
            m_ref[mi] = m_new
            acc_ref[mi] = (jnp.concatenate([alpha, alpha], axis=1) * acc_ref[mi]
                           + jnp.dot(p.astype(BF16), vj, preferred_element_type=F32))

    n_prev = qi
    last = jnp.maximum(n_prev - 1, 0)
    qk_scores(sa_ref, mxa_ref, qi, bias_ref[:, t:])
    qk_scores(sb_ref, mxb_ref, 0)
    first_tile(sa_ref, mxa_ref, qi)

    def pair_body(i, carry):
        qk_scores(sa_ref, mxa_ref, jnp.minimum(2 * i + 1, last))
        next_tile(sb_ref, mxb_ref, 2 * i)
        qk_scores(sb_ref, mxb_ref, jnp.minimum(2 * i + 2, last))
        next_tile(sa_ref, mxa_ref, 2 * i + 1)
        return carry

    lax.fori_loop(0, n_prev // 2, pair_body, 0)

    @pl.when(n_prev % 2 == 1)
    def _():
        next_tile(sb_ref, mxb_ref, last)

    lam_init = scal_ref[0]
    lam = (jnp.exp(jnp.sum(lq1_ref[...] * lk1_ref[...], axis=1, keepdims=True))
           - jnp.exp(jnp.sum(lq2_ref[...] * lk2_ref[...], axis=1, keepdims=True))
           + lam_init)
    o = (acc_ref[0, :, :HEAD_DV] / acc_ref[0, :, HEAD_DV:]
         - lam * (acc_ref[1, :, :HEAD_DV] / acc_ref[1, :, HEAD_DV:]))
    o = o * lax.rsqrt(jnp.mean(o * o, axis=-1, keepdims=True) + EPS) * sg_ref[...]
    o = o * (1.0 - lam_init)
    z = z_ref[...].astype(F32)
    o_ref[...] = (o * (z * jax.nn.sigmoid(z))).astype(o_ref.dtype)


def _attention(proj, scal, rel_bias, lq1, lk1, lq2, lk2, subln_g):
    bsz, seq, _ = proj.shape
    t = ATT_T
    smem = pl.BlockSpec(memory_space=pltpu.SMEM)
    vec = lambda n: pl.BlockSpec((1, n), lambda h, b, i: (0, 0))
    return pl.pallas_call(
        _attn_kernel,
        grid=(N_HEADS, bsz, seq // t),
        in_specs=[smem, smem, vec(HEAD_DK), vec(HEAD_DK), vec(HEAD_DK), vec(HEAD_DK),
                  vec(HEAD_DV),
                  pl.BlockSpec((None, t, HEAD_DV), lambda h, b, i: (b, i, h)),
                  pl.BlockSpec((None, seq, HEAD_DV), lambda h, b, i: (b, 0, N_HEADS + h)),
                  pl.BlockSpec((None, seq, HEAD_DV), lambda h, b, i: (b, 0, 2 * N_HEADS + h)),
                  pl.BlockSpec((None, t, HEAD_DV), lambda h, b, i: (b, i, 3 * N_HEADS + h))],
        out_specs=pl.BlockSpec((None, t, HEAD_DV), lambda h, b, i: (b, i, h)),
        out_shape=jax.ShapeDtypeStruct((bsz, seq, D_ATT), BF16),
        scratch_shapes=[pltpu.VMEM((t, 2 * t), F32),
                        pltpu.VMEM((2, t, LANES), F32),
                        pltpu.VMEM((2, t, 2 * HEAD_DV), F32),
                        pltpu.VMEM((seq, 2 * HEAD_DV), BF16),
                        pltpu.VMEM((2, t, HEAD_DV), BF16),
                        pltpu.VMEM((2, t, t), F32),
                        pltpu.VMEM((2, t, t), F32),
                        pltpu.VMEM((2, t, LANES), F32),
                        pltpu.VMEM((2, t, LANES), F32)],
        compiler_params=pltpu.CompilerParams(
            dimension_semantics=("arbitrary", "arbitrary", "arbitrary"),
            vmem_limit_bytes=VMEM_LIMIT),
        name="diff_attention",
    )(scal, rel_bias, lq1, lk1, lq2, lk2, subln_g, proj, proj, proj, proj)


def _attn_kernel_mh(scal_ref, relb_ref, lq1_ref, lk1_ref, lq2_ref, lk2_ref, sg_ref,
                    q_ref, k_ref, v_ref, z_ref, o_ref,
                    bias_ref, m_ref, acc_ref, vx_ref, qm_ref, sa_ref, sb_ref, mxa_ref, mxb_ref):
    t = ATT_T
    n_lane_tiles = t // LANES
    heads = range(ATT_HEADS)
    hp = pl.program_id(0)
    b = pl.program_id(1)
    qi = pl.program_id(2)
    lanes_of = lambda hh: slice(hh * HEAD_DV, (hh + 1) * HEAD_DV)

    @pl.when((b == 0) & (qi == 0))
    def _():
        w_rows, w_cols = LANES, 3 * LANES
        row = lax.broadcasted_iota(jnp.int32, (w_rows, w_cols), 0)
        col = lax.broadcasted_iota(jnp.int32, (w_rows, w_cols), 1) - LANES
        rel = col - row
        half = N_BUCKETS // 2
        max_exact = half // 2
        n = jnp.abs(rel)
        nf = jnp.maximum(n, 1).astype(F32)
        large = max_exact + (jnp.log(nf / max_exact) / math.log(MAX_DISTANCE / max_exact)
                             * (half - max_exact)).astype(jnp.int32)
        large = jnp.minimum(large, half - 1)
        bucket = jnp.where(rel > 0, half, 0) + jnp.where(n < max_exact, n, large)
        allowed = ((col + LANES) // CHUNK - LANES // CHUNK) <= (row // CHUNK)
        for hh in heads:
            h = hp * ATT_HEADS + hh
            bias = jnp.zeros((w_rows, w_cols), F32)
            for i in range(N_BUCKETS):
                bias = jnp.where(bucket == i, relb_ref[i, h], bias)
            far = relb_ref[half - 1, h]
            window = jnp.where(allowed, (bias - far) * LOG2E, NEG_INF)
            for i in range(n_lane_tiles):
                rows = slice(i * LANES, (i + 1) * LANES)
                lo = (n_lane_tiles + i - 1) * LANES
                hi = min(lo + w_cols, 2 * t)
                bias_ref[hh, rows, :lo] = jnp.zeros((LANES, lo), F32)
                bias_ref[hh, rows, lo:hi] = window[:, :hi - lo]
                if hi < 2 * t:
                    bias_ref[hh, rows, hi:] = jnp.full((LANES, 2 * t - hi), NEG_INF, F32)

    @pl.when(qi == 0)
    def _():
        for hh in heads:
            vx_ref[hh, :, :HEAD_DV] = v_ref[:, lanes_of(hh)]
            vx_ref[hh, :, HEAD_DV:] = jnp.ones((v_ref.shape[0], HEAD_DV), BF16)

    lane = lax.broadcasted_iota(jnp.int32, (t, HEAD_DV), 1)
    for hh in heads:
        q = q_ref[:, lanes_of(hh)]
        zero = jnp.zeros_like(q)
        qm_ref[hh, 0] = jnp.where(lane < HEAD_DK, q, zero)
        qm_ref[hh, 1] = jnp.where(lane >= HEAD_DK, q, zero)

    def qk_scores(hh, s_ref, mx_ref, tile, diagonal=False):
        kj = k_ref[pl.ds(pl.multiple_of(tile * t, t), t), lanes_of(hh)]
        if diagonal:
            bias = bias_ref[hh, :, t:]
        else:
            corner = jnp.where(tile == qi - 1, bias_ref[hh, :LANES, t - LANES:t], 0.0)
        for mi in range(2):
            s = lax.dot_general(qm_ref[hh, mi], kj, (((1,), (1,)), ((), ())),
                                preferred_element_type=F32)
            if diagonal:
                s = s + bias
            else:
                top = jnp.concatenate([s[:LANES, :t - LANES], s[:LANES, t - LANES:] + corner],
                                      axis=1)
                s = jnp.concatenate([top, s[LANES:]], axis=0)
            s_ref[hh, mi] = s
            mx_ref[hh, mi] = jnp.broadcast_to(jnp.max(s, axis=1, keepdims=True), (t, LANES))

    def first_tile(hh, s_ref, mx_ref, tile):
        vj = vx_ref[hh, pl.ds(pl.multiple_of(tile * t, t), t), :]
        for mi in range(2):
            m_new = mx_ref[hh, mi]
            p = jnp.exp2(s_ref[hh, mi] - jnp.concatenate([m_new] * n_lane_tiles, axis=1))
            m_ref[hh, mi] = m_new
            acc_ref[hh, mi] = jnp.dot(p.astype(BF16), vj, preferred_element_type=F32)

    def next_tile(hh, s_ref, mx_ref, tile):
        vj = vx_ref[hh, pl.ds(pl.multiple_of(tile * t, t), t), :]
        for mi in range(2):
            m_prev = m_ref[hh, mi]
            m_new = jnp.maximum(m_prev, mx_ref[hh, mi])
            alpha = jnp.exp2(m_prev - m_new)
            p = jnp.exp2(s_ref[hh, mi] - jnp.concatenate([m_new] * n_lane_tiles, axis=1))
            m_ref[hh, mi] = m_new
            acc_ref[hh, mi] = (jnp.concatenate([alpha, alpha], axis=1) * acc_ref[hh, mi]
                               + jnp.dot(p.astype(BF16), vj, preferred_element_type=F32))

    n_prev = qi
    last = jnp.maximum(n_prev - 1, 0)
    lam_init = scal_ref[0]
    lam = (jnp.exp(jnp.sum(lq1_ref[...] * lk1_ref[...], axis=1, keepdims=True))
           - jnp.exp(jnp.sum(lq2_ref[...] * lk2_ref[...], axis=1, keepdims=True))
           + lam_init)

    for hh in heads:
        qk_scores(hh, sa_ref, mxa_ref, qi, diagonal=True)
        qk_scores(hh, sb_ref, mxb_ref, 0)
        first_tile(hh, sa_ref, mxa_ref, qi)

        def pair_body(i, carry, hh=hh):
            qk_scores(hh, sa_ref, mxa_ref, jnp.minimum(2 * i + 1, last))
            next_tile(hh, sb_ref, mxb_ref, 2 * i)
            qk_scores(hh, sb_ref, mxb_ref, jnp.minimum(2 * i + 2, last))
            next_tile(hh, sa_ref, mxa_ref, 2 * i + 1)
            return carry

        lax.fori_loop(0, n_prev // 2, pair_body, 0)

        @pl.when(n_prev % 2 == 1)
        def _(hh=hh):
            next_tile(hh, sb_ref, mxb_ref, last)

        o = (acc_ref[hh, 0, :, :HEAD_DV] / acc_ref[hh, 0, :, HEAD_DV:]
             - lam * (acc_ref[hh, 1, :, :HEAD_DV] / acc_ref[hh, 1, :, HEAD_DV:]))
        o = o * lax.rsqrt(jnp.mean(o * o, axis=-1, keepdims=True) + EPS) * sg_ref[...]
        o = o * (1.0 - lam_init)
        z = z_ref[:, lanes_of(hh)].astype(F32)
        o_ref[:, lanes_of(hh)] = (o * (z * jax.nn.sigmoid(z))).astype(o_ref.dtype)


def _attention_mh(proj, scal, rel_bias, lq1, lk1, lq2, lk2, subln_g):
    bsz, seq, _ = proj.shape
    t = ATT_T
    nh = ATT_HEADS
    width = nh * HEAD_DV
    groups = N_HEADS // nh
    smem = pl.BlockSpec(memory_space=pltpu.SMEM)
    vec = lambda n: pl.BlockSpec((1, n), lambda g, b, i: (0, 0))
    return pl.pallas_call(
        _attn_kernel_mh,
        grid=(groups, bsz, seq // t),
        in_specs=[smem, smem, vec(HEAD_DK), vec(HEAD_DK), vec(HEAD_DK), vec(HEAD_DK),
                  vec(HEAD_DV),
                  pl.BlockSpec((None, t, width), lambda g, b, i: (b, i, g)),
                  pl.BlockSpec((None, seq, width), lambda g, b, i: (b, 0, groups + g)),
                  pl.BlockSpec((None, seq, width), lambda g, b, i: (b, 0, 2 * groups + g)),
                  pl.BlockSpec((None, t, width), lambda g, b, i: (b, i, 3 * groups + g))],
        out_specs=pl.BlockSpec((None, t, width), lambda g, b, i: (b, i, g)),
        out_shape=jax.ShapeDtypeStruct((bsz, seq, D_ATT), BF16),
        scratch_shapes=[pltpu.VMEM((nh, t, 2 * t), F32),
                        pltpu.VMEM((nh, 2, t, LANES), F32),
                        pltpu.VMEM((nh, 2, t, 2 * HEAD_DV), F32),
                        pltpu.VMEM((nh, seq, 2 * HEAD_DV), BF16),
                        pltpu.VMEM((nh, 2, t, HEAD_DV), BF16),
                        pltpu.VMEM((nh, 2, t, t), F32),
                        pltpu.VMEM((nh, 2, t, t), F32),
                        pltpu.VMEM((nh, 2, t, LANES), F32),
                        pltpu.VMEM((nh, 2, t, LANES), F32)],
        compiler_params=pltpu.CompilerParams(
            dimension_semantics=("arbitrary", "arbitrary", "arbitrary"),
            vmem_limit_bytes=VMEM_LIMIT),
        name="diff_attention",
    )(scal, rel_bias, lq1, lk1, lq2, lk2, subln_g, proj, proj, proj, proj)


def _gelu_tanh(x):
    return 0.5 * x * (1.0 + jnp.tanh(math.sqrt(2.0 / math.pi) * (x + 0.044715 * (x * x * x))))


def _ssm_kernel(u_ref, z_ref, bw_ref, cw_ref, are_ref, aim_ref, d_ref, wglu_ref,
                o_ref, hre_ref, him_ref, st_ref, g_ref):
    nb = u_ref.shape[0]
    tb = SSM_TB
    pitch = SSM_PITCH
    pairs_per_chunk = MXU_DIM // (2 * SSM_GROUP)
    n_vregs = N_PAIRS // SUBLANES

    def pair_rows(b, gp):
        return slice((b * N_PAIRS + gp) * pitch, (b * N_PAIRS + gp) * pitch + tb)

    @pl.when(pl.program_id(0) == 0)
    def _():
        st_ref[...] = jnp.zeros_like(st_ref)

    for gp in range(N_PAIRS):
        kc = gp // pairs_per_chunk
        uc = jnp.concatenate([u_ref[b, :, kc * MXU_DIM:(kc + 1) * MXU_DIM] for b in range(nb)],
                             axis=0)
        bu = jnp.dot(uc, bw_ref[gp], preferred_element_type=F32)
        for b in range(nb):
            hre_ref[pair_rows(b, gp), :] = bu[b * tb:(b + 1) * tb, :LANES]
            him_ref[pair_rows(b, gp), :] = bu[b * tb:(b + 1) * tb, LANES:]

    a_re = [are_ref[k] for k in range(n_vregs)]
    a_im = [aim_ref[k] for k in range(n_vregs)]

    def step(ti, carry):
        out = []
        for b in range(nb):
            for k in range(n_vregs):
                h_re, h_im = carry[2 * (b * n_vregs + k)], carry[2 * (b * n_vregs + k) + 1]
                rows = pl.ds((b * N_PAIRS + k * SUBLANES) * pitch + ti, SUBLANES, stride=pitch)
                n_re = a_re[k] * h_re - a_im[k] * h_im + hre_ref[rows, :]
                n_im = a_re[k] * h_im + a_im[k] * h_re + him_ref[rows, :]
                hre_ref[rows, :] = n_re
                him_ref[rows, :] = n_im
                out += [n_re, n_im]
        return tuple(out)

    init = tuple(st_ref[i] for i in range(2 * nb * n_vregs))
    final = lax.fori_loop(0, tb, step, init, unroll=8)
    for i, s in enumerate(final):
        st_ref[i] = s

    for kc in range(D_SSM // MXU_DIM):
        gps = range(kc * pairs_per_chunk, (kc + 1) * pairs_per_chunk)
        hcat = jnp.concatenate(
            [jnp.concatenate([ref[pair_rows(b, gp), :]
                              for gp in gps for ref in (hre_ref, him_ref)], axis=1)
             for b in range(nb)], axis=0).astype(BF16)
        cwk = cw_ref[kc * pairs_per_chunk:(kc + 1) * pairs_per_chunk]
        acc = jnp.dot(hcat, cwk.reshape(pairs_per_chunk * MXU_DIM, MXU_DIM),
                      preferred_element_type=F32)
        cols = slice(kc * MXU_DIM, (kc + 1) * MXU_DIM)
        for b in range(nb):
            y = acc[b * tb:(b + 1) * tb] + d_ref[:, cols] * u_ref[b, :, cols].astype(F32)
            g_ref[b * tb:(b + 1) * tb, cols] = _gelu_tanh(y).astype(BF16)

    glu = jnp.dot(g_ref[...], wglu_ref[...], preferred_element_type=F32)
    for b in range(nb):
        gb = glu[b * tb:(b + 1) * tb]
        z = z_ref[b].astype(F32)
        o = gb[:, :D_SSM] * jax.nn.sigmoid(gb[:, D_SSM:]) * (z * jax.nn.sigmoid(z))
        o_ref[b] = o.astype(o_ref.dtype)


def _ssm(proj, bw, cw, a_re_v, a_im_v, d_skip, w_glu):
    bsz, seq, _ = proj.shape
    tb = SSM_TB
    const = lambda shape: pl.BlockSpec(shape, lambda i: (0,) * len(shape),
                                       pipeline_mode=pl.Buffered(1))
    u_col = 4 * D_ATT // D_SSM
    n_state_vregs = 2 * bsz * (N_PAIRS // SUBLANES)
    return pl.pallas_call(
        _ssm_kernel,
        grid=(seq // tb,),
        in_specs=[pl.BlockSpec((bsz, tb, D_SSM), lambda i: (0, i, u_col)),
                  pl.BlockSpec((bsz, tb, D_SSM), lambda i: (0, i, u_col + 1)),
                  const((N_PAIRS, MXU_DIM, MXU_DIM)),
                  const((N_PAIRS, MXU_DIM, MXU_DIM)),
                  const((N_PAIRS // SUBLANES, SUBLANES, LANES)),
                  const((N_PAIRS // SUBLANES, SUBLANES, LANES)),
                  const((1, D_SSM)),
                  const((D_SSM, 2 * D_SSM))],
        out_specs=pl.BlockSpec((bsz, tb, D_SSM), lambda i: (0, i, 0)),
        out_shape=jax.ShapeDtypeStruct((bsz, seq, D_SSM), BF16),
        scratch_shapes=[pltpu.VMEM((bsz * N_PAIRS * SSM_PITCH, LANES), F32),
                        pltpu.VMEM((bsz * N_PAIRS * SSM_PITCH, LANES), F32),
                        pltpu.VMEM((n_state_vregs, SUBLANES, LANES), F32),
                        pltpu.VMEM((bsz * tb, D_SSM), BF16)],
        compiler_params=pltpu.CompilerParams(
            dimension_semantics=("arbitrary",),
            vmem_limit_bytes=VMEM_LIMIT),
        name="s5_glu",
    )(proj, proj, bw, cw, a_re_v, a_im_v, d_skip, w_glu)


def _ssm_weights(ab_re, ab_im, bb_re, bb_im, c_re, c_im):
    ppc = MXU_DIM // (2 * SSM_GROUP)
    sel = (jnp.arange(N_PAIRS)[:, None] % ppc == jnp.arange(ppc)[None, :]).astype(F32)
    group = jnp.arange(2)[None, :, None, None]

    def place(x_re, x_im):
        pieces = []
        for x in (x_re, x_im):
            xg = x.reshape(N_PAIRS, 2, SSM_GROUP, STATE)
            for gl2 in range(2):
                piece = jnp.where(group == gl2, xg, 0.0)[:, None] * sel[:, :, None, None, None]
                pieces.append(piece.reshape(N_PAIRS, ppc, 2 * SSM_GROUP, STATE))
        w = jnp.concatenate(pieces, axis=-1)
        return w.reshape(N_PAIRS, MXU_DIM, MXU_DIM)

    bw = place(bb_re, bb_im).astype(BF16)
    cw = jnp.swapaxes(place(c_re, -c_im), 1, 2).astype(BF16)
    vshape = (N_PAIRS // SUBLANES, SUBLANES, LANES)
    return bw, cw, ab_re.reshape(vshape), ab_im.reshape(vshape)


def _out_proj_kernel(oa_ref, os_ref, w_ref, g_ref, x_ref, o_ref):
    mix = (jnp.dot(oa_ref[...], w_ref[:D_ATT, :], preferred_element_type=F32)
           + jnp.dot(os_ref[...], w_ref[D_ATT:, :], preferred_element_type=F32))
    y = mix * lax.rsqrt(jnp.mean(mix * mix, axis=-1, keepdims=True) + EPS)
    o_ref[...] = x_ref[...] + y * g_ref[...]


def _out_proj(o_att, o_ssm, w, g, x2d):
    m = x2d.shape[0]
    return pl.pallas_call(
        _out_proj_kernel,
        grid=(m // OUT_TM,),
        in_specs=[pl.BlockSpec((OUT_TM, D_ATT), lambda i: (i, 0)),
                  pl.BlockSpec((OUT_TM, D_SSM), lambda i: (i, 0)),
                  pl.BlockSpec((D_MODEL, D_MODEL), lambda i: (0, 0),
                               pipeline_mode=pl.Buffered(1)),
                  pl.BlockSpec((1, D_MODEL), lambda i: (0, 0)),
                  pl.BlockSpec((OUT_TM, D_MODEL), lambda i: (i, 0))],
        out_specs=pl.BlockSpec((OUT_TM, D_MODEL), lambda i: (i, 0)),
        out_shape=jax.ShapeDtypeStruct((m, D_MODEL), F32),
        compiler_params=pltpu.CompilerParams(
            dimension_semantics=("arbitrary",),
            vmem_limit_bytes=VMEM_LIMIT),
        name="out_proj",
    )(o_att, o_ssm, w, g, x2d)


def kernel(x, rel_bias, pre_norm_g, post_norm_g, w_in, lambda_q1, lambda_k1, lambda_q2,
           lambda_k2, subln_g, ssm_a_re, ssm_a_im, ssm_log_dt, ssm_b_re, ssm_b_im,
           ssm_c_re, ssm_c_im, ssm_d, w_glu, w_out):
    bsz, seq, _ = x.shape
    m = bsz * seq
    ab_re, ab_im, bb_re, bb_im = _discretize(ssm_a_re, ssm_a_im, ssm_log_dt, ssm_b_re, ssm_b_im)
    col_scale = jnp.concatenate([jnp.full((D_ATT,), HEAD_DK ** -0.5 * LOG2E, F32),
                                 jnp.ones((D_IN - D_ATT,), F32)])
    x2d = x.reshape(m, D_MODEL)
    for l in range(DEPTH):
        lam_init = _lambda_init(l)
        w_in_l = (w_in[l] * col_scale).astype(BF16)
        proj = _in_proj(x2d, pre_norm_g[l].reshape(1, D_MODEL), w_in_l)
        proj = proj.reshape(bsz, seq, D_IN)
        scal = jnp.array([lam_init], F32)
        o_att = _attention_mh(proj, scal, rel_bias,
                           lambda_q1[l].reshape(1, HEAD_DK), lambda_k1[l].reshape(1, HEAD_DK),
                           lambda_q2[l].reshape(1, HEAD_DK), lambda_k2[l].reshape(1, HEAD_DK),
                           subln_g[l].reshape(1, HEAD_DV))
        bw, cw, a_re_v, a_im_v = _ssm_weights(
            ab_re[l, :, 0, :], ab_im[l, :, 0, :], bb_re[l], bb_im[l], ssm_c_re[l], ssm_c_im[l])
        o_ssm = _ssm(proj, bw, cw, a_re_v, a_im_v, ssm_d[l].reshape(1, D_SSM),
                     w_glu[l].astype(BF16))
        x2d = _out_proj(o_att.reshape(m, D_ATT), o_ssm.reshape(m, D_SSM),
                        w_out[l].astype(BF16), post_norm_g[l].reshape(1, D_MODEL), x2d)
    return x2d.reshape(bsz, seq, D_MODEL)
```

```python
import math

import jax
import jax.numpy as jnp
from jax import lax
from jax.experimental import pallas as pl
from jax.experimental.pallas import tpu as pltpu

F32 = jnp.float32
BF16 = jnp.bfloat16

D_MODEL = 2048
DEPTH = 4
CHUNK = 64
D_ATT = 1024
D_SSM = 1024
N_HEADS = 8
HEAD_DV = 128
HEAD_DK = 64
SSM_GROUP = 16
N_GROUPS = 64
STATE = 64
N_BUCKETS = 32
MAX_DISTANCE = 128
EPS = 1e-6
NEG_INF = -1e30
LOG2E = math.log2(math.e)
D_IN = 4 * D_ATT + 2 * D_SSM

LANES = 128
SUBLANES = 8
MXU_DIM = 256
VMEM_LIMIT = 56 * 1024 * 1024

PROJ_TM = 1024
PROJ_TN = 1024
ATT_T = 512
ATT_HEADS = 2
VT_ROWS = HEAD_DV + 16
SSM_TB = 256
SSM_PITCH = SSM_TB + SUBLANES // 2
N_PAIRS = N_GROUPS // 2
OUT_TM = 512


def _lambda_init(layer_idx):
    return 0.8 - 0.6 * math.exp(-0.3 * layer_idx)


def _discretize_kernel(a_re_ref, a_im_ref, log_dt_ref, b_re_ref, b_im_ref,
                       ab_re_ref, ab_im_ref, bb_re_ref, bb_im_ref):
    a_re = a_re_ref[...]
    a_im = a_im_ref[...]
    dt = jnp.exp(log_dt_ref[...])
    mag = jnp.exp(dt * a_re)
    ab_re = mag * jnp.cos(dt * a_im)
    ab_im = mag * jnp.sin(dt * a_im)
    den = a_re * a_re + a_im * a_im
    nr = ab_re - 1.0
    cf_re = (nr * a_re + ab_im * a_im) / den
    cf_im = (ab_im * a_re - nr * a_im) / den
    b_re = b_re_ref[...]
    b_im = b_im_ref[...]
    ab_re_ref[...] = ab_re
    ab_im_ref[...] = ab_im
    bb_re_ref[...] = cf_re * b_re - cf_im * b_im
    bb_im_ref[...] = cf_re * b_im + cf_im * b_re


def _discretize(a_re, a_im, log_dt, b_re, b_im):
    nl = a_re.shape[0]
    a4 = (nl, N_GROUPS, 1, STATE)
    b4 = (nl, N_GROUPS, SSM_GROUP, STATE)
    return pl.pallas_call(
        _discretize_kernel,
        out_shape=(jax.ShapeDtypeStruct(a4, F32), jax.ShapeDtypeStruct(a4, F32),
                   jax.ShapeDtypeStruct(b4, F32), jax.ShapeDtypeStruct(b4, F32)),
        name="s5_discretize",
    )(a_re.reshape(a4), a_im.reshape(a4), log_dt.reshape(nl, N_GROUPS, 1, 1),
      jnp.swapaxes(b_re, 2, 3), jnp.swapaxes(b_im, 2, 3))


def _in_proj_kernel(x_ref, g_ref, w_ref, o_ref, h_ref):
    @pl.when(pl.program_id(1) == 0)
    def _():
        x = x_ref[...]
        y = x * lax.rsqrt(jnp.mean(x * x, axis=-1, keepdims=True) + EPS)
        h_ref[...] = (y * g_ref[...]).astype(BF16)

    o_ref[...] = jnp.dot(h_ref[...], w_ref[...],
                         preferred_element_type=F32).astype(o_ref.dtype)


def _in_proj(x2d, g, w):
    m = x2d.shape[0]
    return pl.pallas_call(
        _in_proj_kernel,
        grid=(m // PROJ_TM, D_IN // PROJ_TN),
        in_specs=[pl.BlockSpec((PROJ_TM, D_MODEL), lambda i, j: (i, 0)),
                  pl.BlockSpec((1, D_MODEL), lambda i, j: (0, 0)),
                  pl.BlockSpec((D_MODEL, PROJ_TN), lambda i, j: (0, j))],
        out_specs=pl.BlockSpec((PROJ_TM, PROJ_TN), lambda i, j: (i, j)),
        out_shape=jax.ShapeDtypeStruct((m, D_IN), BF16),
        scratch_shapes=[pltpu.VMEM((PROJ_TM, D_MODEL), BF16)],
        compiler_params=pltpu.CompilerParams(
            dimension_semantics=("arbitrary", "arbitrary"),
            vmem_limit_bytes=VMEM_LIMIT),
        name="in_proj",
    )(x2d, g, w)


def _attn_kernel(scal_ref, relb_ref, lq1_ref, lk1_ref, lq2_ref, lk2_ref, sg_ref,
                 q_ref, k_ref, v_ref, z_ref, o_ref,
                 bias_ref, m_ref, acc_ref, vx_ref, qm_ref, sa_ref, sb_ref, mxa_ref, mxb_ref):
    t = ATT_T
    n_lane_tiles = t // LANES
    h = pl.program_id(0)
    b = pl.program_id(1)
    qi = pl.program_id(2)

    @pl.when((b == 0) & (qi == 0))
    def _():
        w_rows, w_cols = LANES, 3 * LANES
        row = lax.broadcasted_iota(jnp.int32, (w_rows, w_cols), 0)
        col = lax.broadcasted_iota(jnp.int32, (w_rows, w_cols), 1) - LANES
        rel = col - row
        half = N_BUCKETS // 2
        max_exact = half // 2
        n = jnp.abs(rel)
        nf = jnp.maximum(n, 1).astype(F32)
        large = max_exact + (jnp.log(nf / max_exact) / math.log(MAX_DISTANCE / max_exact)
                             * (half - max_exact)).astype(jnp.int32)
        large = jnp.minimum(large, half - 1)
        bucket = jnp.where(rel > 0, half, 0) + jnp.where(n < max_exact, n, large)
        bias = jnp.zeros((w_rows, w_cols), F32)
        for i in range(N_BUCKETS):
            bias = jnp.where(bucket == i, relb_ref[i, h], bias)
        far = relb_ref[half - 1, h]
        allowed = ((col + LANES) // CHUNK - LANES // CHUNK) <= (row // CHUNK)
        window = jnp.where(allowed, (bias - far) * LOG2E, NEG_INF)
        for i in range(n_lane_tiles):
            rows = slice(i * LANES, (i + 1) * LANES)
            lo = (n_lane_tiles + i - 1) * LANES
            hi = min(lo + w_cols, 2 * t)
            bias_ref[rows, :lo] = jnp.zeros((LANES, lo), F32)
            bias_ref[rows, lo:hi] = window[:, :hi - lo]
            if hi < 2 * t:
                bias_ref[rows, hi:] = jnp.full((LANES, 2 * t - hi), NEG_INF, F32)

    @pl.when(qi == 0)
    def _():
        vx_ref[:, :HEAD_DV] = v_ref[...]
        vx_ref[:, HEAD_DV:] = jnp.ones((v_ref.shape[0], HEAD_DV), BF16)

    q = q_ref[...]
    lane = lax.broadcasted_iota(jnp.int32, (t, HEAD_DV), 1)
    zero = jnp.zeros_like(q)
    qm_ref[0] = jnp.where(lane < HEAD_DK, q, zero)
    qm_ref[1] = jnp.where(lane >= HEAD_DK, q, zero)

    def qk_scores(s_ref, mx_ref, tile, bias=None):
        kj = k_ref[pl.ds(pl.multiple_of(tile * t, t), t), :]
        if bias is None:
            corner = jnp.where(tile == qi - 1, bias_ref[:LANES, t - LANES:t], 0.0)
        for mi in range(2):
            s = lax.dot_general(qm_ref[mi], kj, (((1,), (1,)), ((), ())),
                                preferred_element_type=F32)
            if bias is None:
                top = jnp.concatenate([s[:LANES, :t - LANES], s[:LANES, t - LANES:] + corner],
                                      axis=1)
                s = jnp.concatenate([top, s[LANES:]], axis=0)
            else:
                s = s + bias
            s_ref[mi] = s
            mx_ref[mi] = jnp.broadcast_to(jnp.max(s, axis=1, keepdims=True), (t, LANES))

    def first_tile(s_ref, mx_ref, tile):
        vj = vx_ref[pl.ds(pl.multiple_of(tile * t, t), t), :]
        for mi in range(2):
            m_new = mx_ref[mi]
            p = jnp.exp2(s_ref[mi] - jnp.concatenate([m_new] * n_lane_tiles, axis=1))
            m_ref[mi] = m_new
            acc_ref[mi] = jnp.dot(p.astype(BF16), vj, preferred_element_type=F32)

    def next_tile(s_ref, mx_ref, tile):
        vj = vx_ref[pl.ds(pl.multiple_of(tile * t, t), t), :]
        for mi in range(2):
            m_prev = m_ref[mi]
            m_new = jnp.maximum(m_prev, mx_ref[mi])
            alpha = jnp.exp2(m_prev - m_new)
            p = jnp.exp2(s_ref[mi] - jnp.concatenate([m_new] * n_lane_tiles, axis=1))
            m_ref[mi] = m_new
            acc_ref[mi] = (jnp.concatenate([alpha, alpha], axis=1) * acc_ref[mi]
                           + jnp.dot(p.astype(BF16), vj, preferred_element_type=F32))

    n_prev = qi
    last = jnp.maximum(n_prev - 1, 0)
    qk_scores(sa_ref, mxa_ref, qi, bias_ref[:, t:])
    qk_scores(sb_ref, mxb_ref, 0)
    first_tile(sa_ref, mxa_ref, qi)

    def pair_body(i, carry):
        qk_scores(sa_ref, mxa_ref, jnp.minimum(2 * i + 1, last))
        next_tile(sb_ref, mxb_ref, 2 * i)
        qk_scores(sb_ref, mxb_ref, jnp.minimum(2 * i + 2, last))
        next_tile(sa_ref, mxa_ref, 2 * i + 1)
        return carry

    lax.fori_loop(0, n_prev // 2, pair_body, 0)

    @pl.when(n_prev % 2 == 1)
    def _():
        next_tile(sb_ref, mxb_ref, last)

    lam_init = scal_ref[0]
    lam = (jnp.exp(jnp.sum(lq1_ref[...] * lk1_ref[...], axis=1, keepdims=True))
           - jnp.exp(jnp.sum(lq2_ref[...] * lk2_ref[...], axis=1, keepdims=True))
           + lam_init)
    o = (acc_ref[0, :, :HEAD_DV] / acc_ref[0, :, HEAD_DV:]
         - lam * (acc_ref[1, :, :HEAD_DV] / acc_ref[1, :, HEAD_DV:]))
    o = o * lax.rsqrt(jnp.mean(o * o, axis=-1, keepdims=True) + EPS) * sg_ref[...]
    o = o * (1.0 - lam_init)
    z = z_ref[...].astype(F32)
    o_ref[...] = (o * (z * jax.nn.sigmoid(z))).astype(o_ref.dtype)


def _attention(proj, scal, rel_bias, lq1, lk1, lq2, lk2, subln_g):
    bsz, seq, _ = proj.shape
    t = ATT_T
    smem = pl.BlockSpec(memory_space=pltpu.SMEM)
    vec = lambda n: pl.BlockSpec((1, n), lambda h, b, i: (0, 0))
    return pl.pallas_call(
        _attn_kernel,
        grid=(N_HEADS, bsz, seq // t),
        in_specs=[smem, smem, vec(HEAD_DK), vec(HEAD_DK), vec(HEAD_DK), vec(HEAD_DK),
                  vec(HEAD_DV),
                  pl.BlockSpec((None, t, HEAD_DV), lambda h, b, i: (b, i, h)),
                  pl.BlockSpec((None, seq, HEAD_DV), lambda h, b, i: (b, 0, N_HEADS + h)),
                  pl.BlockSpec((None, seq, HEAD_DV), lambda h, b, i: (b, 0, 2 * N_HEADS + h)),
                  pl.BlockSpec((None, t, HEAD_DV), lambda h, b, i: (b, i, 3 * N_HEADS + h))],
        out_specs=pl.BlockSpec((None, t, HEAD_DV), lambda h, b, i: (b, i, h)),
        out_shape=jax.ShapeDtypeStruct((bsz, seq, D_ATT), BF16),
        scratch_shapes=[pltpu.VMEM((t, 2 * t), F32),
                        pltpu.VMEM((2, t, LANES), F32),
                        pltpu.VMEM((2, t, 2 * HEAD_DV), F32),
                        pltpu.VMEM((seq, 2 * HEAD_DV), BF16),
                        pltpu.VMEM((2, t, HEAD_DV), BF16),
                        pltpu.VMEM((2, t, t), F32),
                        pltpu.VMEM((2, t, t), F32),
                        pltpu.VMEM((2, t, LANES), F32),
                        pltpu.VMEM((2, t, LANES), F32)],
        compiler_params=pltpu.CompilerParams(
            dimension_semantics=("arbitrary", "arbitrary", "arbitrary"),
            vmem_limit_bytes=VMEM_LIMIT),
        name="diff_attention",
    )(scal, rel_bias, lq1, lk1, lq2, lk2, subln_g, proj, proj, proj, proj)


def _attn_kernel_mh(scal_ref, relb_ref, lq1_ref, lk1_ref, lq2_ref, lk2_ref, sg_ref,
                    q_ref, k_ref, v_ref, z_ref, o_ref,
                    bias_ref, m_ref, acc_ref, vx_ref, qm_ref, sa_ref, sb_ref, mxa_ref, mxb_ref):
    t = ATT_T
    n_lane_tiles = t // LANES
    heads = range(ATT_HEADS)
    hp = pl.program_id(0)
    b = pl.program_id(1)
    qi = pl.program_id(2)
    lanes_of = lambda hh: slice(hh * HEAD_DV, (hh + 1) * HEAD_DV)

    @pl.when((b == 0) & (qi == 0))
    def _():
        w_rows, w_cols = LANES, 3 * LANES
        row = lax.broadcasted_iota(jnp.int32, (w_rows, w_cols), 0)
        col = lax.broadcasted_iota(jnp.int32, (w_rows, w_cols), 1) - LANES
        rel = col - row
        half = N_BUCKETS // 2
        max_exact = half // 2
        n = jnp.abs(rel)
        nf = jnp.maximum(n, 1).astype(F32)
        large = max_exact + (jnp.log(nf / max_exact) / math.log(MAX_DISTANCE / max_exact)
                             * (half - max_exact)).astype(jnp.int32)
        large = jnp.minimum(large, half - 1)
        bucket = jnp.where(rel > 0, half, 0) + jnp.where(n < max_exact, n, large)
        allowed = ((col + LANES) // CHUNK - LANES // CHUNK) <= (row // CHUNK)
        for hh in heads:
            h = hp * ATT_HEADS + hh
            bias = jnp.zeros((w_rows, w_cols), F32)
            for i in range(N_BUCKETS):
                bias = jnp.where(bucket == i, relb_ref[i, h], bias)
            far = relb_ref[half - 1, h]
            window = jnp.where(allowed, (bias - far) * LOG2E, NEG_INF)
            for i in range(n_lane_tiles):
                rows = slice(i * LANES, (i + 1) * LANES)
                lo = (n_lane_tiles + i - 1) * LANES
                hi = min(lo + w_cols, 2 * t)
                bias_ref[hh, rows, :lo] = jnp.zeros((LANES, lo), F32)
                bias_ref[hh, rows, lo:hi] = window[:, :hi - lo]
                if hi < 2 * t:
                    bias_ref[hh, rows, hi:] = jnp.full((LANES, 2 * t - hi), NEG_INF, F32)

    @pl.when(qi == 0)
    def _():
        for hh in heads:
            vx_ref[hh, :, :HEAD_DV] = v_ref[:, lanes_of(hh)]
            vx_ref[hh, :, HEAD_DV:] = jnp.ones((v_ref.shape[0], HEAD_DV), BF16)

    lane = lax.broadcasted_iota(jnp.int32, (t, HEAD_DV), 1)
    for hh in heads:
        q = q_ref[:, lanes_of(hh)]
        zero = jnp.zeros_like(q)
        qm_ref[hh, 0] = jnp.where(lane < HEAD_DK, q, zero)
        qm_ref[hh, 1] = jnp.where(lane >= HEAD_DK, q, zero)

    def qk_scores(hh, s_ref, mx_ref, tile, diagonal=False):
        kj = k_ref[pl.ds(pl.multiple_of(tile * t, t), t), lanes_of(hh)]
        if diagonal:
            bias = bias_ref[hh, :, t:]
        else:
            corner = jnp.where(tile == qi - 1, bias_ref[hh, :LANES, t - LANES:t], 0.0)
        for mi in range(2):
            s = lax.dot_general(qm_ref[hh, mi], kj, (((1,), (1,)), ((), ())),
                                preferred_element_type=F32)
            if diagonal:
                s = s + bias
            else:
                top = jnp.concatenate([s[:LANES, :t - LANES], s[:LANES, t - LANES:] + corner],
                                      axis=1)
                s = jnp.concatenate([top, s[LANES:]], axis=0)
            s_ref[hh, mi] = s
            mx_ref[hh, mi] = jnp.broadcast_to(jnp.max(s, axis=1, keepdims=True), (t, LANES))

    def first_tile(hh, s_ref, mx_ref, tile):
        vj = vx_ref[hh, pl.ds(pl.multiple_of(tile * t, t), t), :]
        for mi in range(2):
            m_new = mx_ref[hh, mi]
            p = jnp.exp2(s_ref[hh, mi] - jnp.concatenate([m_new] * n_lane_tiles, axis=1))
            m_ref[hh, mi] = m_new
            acc_ref[hh, mi] = jnp.dot(p.astype(BF16), vj, preferred_element_type=F32)

    def next_tile(hh, s_ref, mx_ref, tile):
        vj = vx_ref[hh, pl.ds(pl.multiple_of(tile * t, t), t), :]
        for mi in range(2):
            m_prev = m_ref[hh, mi]
            m_new = jnp.maximum(m_prev, mx_ref[hh, mi])
            alpha = jnp.exp2(m_prev - m_new)
            p = jnp.exp2(s_ref[hh, mi] - jnp.concatenate([m_new] * n_lane_tiles, axis=1))
            m_ref[hh, mi] = m_new
            acc_ref[hh, mi] = (jnp.concatenate([alpha, alpha], axis=1) * acc_ref[hh, mi]
                               + jnp.dot(p.astype(BF16), vj, preferred_element_type=F32))

    n_prev = qi
    last = jnp.maximum(n_prev - 1, 0)
    lam_init = scal_ref[0]
    lam = (jnp.exp(jnp.sum(lq1_ref[...] * lk1_ref[...], axis=1, keepdims=True))
           - jnp.exp(jnp.sum(lq2_ref[...] * lk2_ref[...], axis=1, keepdims=True))
           + lam_init)

    for hh in heads:
        qk_scores(hh, sa_ref, mxa_ref, qi, diagonal=True)
        qk_scores(hh, sb_ref, mxb_ref, 0)
        first_tile(hh, sa_ref, mxa_ref, qi)

        def pair_body(i, carry, hh=hh):
            qk_scores(hh, sa_ref, mxa_ref, jnp.minimum(2 * i + 1, last))
            next_tile(hh, sb_ref, mxb_ref, 2 * i)
            qk_scores(hh, sb_ref, mxb_ref, jnp.minimum(2 * i + 2, last))
            next_tile(hh, sa_ref, mxa_ref, 2 * i + 1)
            return carry

        lax.fori_loop(0, n_prev // 2, pair_body, 0)

        @pl.when(n_prev % 2 == 1)
        def _(hh=hh):
            next_tile(hh, sb_ref, mxb_ref, last)

        o = (acc_ref[hh, 0, :, :HEAD_DV] / acc_ref[hh, 0, :, HEAD_DV:]
             - lam * (acc_ref[hh, 1, :, :HEAD_DV] / acc_ref[hh, 1, :, HEAD_DV:]))
        o = o * lax.rsqrt(jnp.mean(o * o, axis=-1, keepdims=True) + EPS) * sg_ref[...]
        o = o * (1.0 - lam_init)
        z = z_ref[:, lanes_of(hh)].astype(F32)
        o_ref[:, lanes_of(hh)] = (o * (z * jax.nn.sigmoid(z))).astype(o_ref.dtype)


def _attention_mh(proj, scal, rel_bias, lq1, lk1, lq2, lk2, subln_g):
    bsz, seq, _ = proj.shape
    t = ATT_T
    nh = ATT_HEADS
    width = nh * HEAD_DV
    groups = N_HEADS // nh
    smem = pl.BlockSpec(memory_space=pltpu.SMEM)
    vec = lambda n: pl.BlockSpec((1, n), lambda g, b, i: (0, 0))
    return pl.pallas_call(
        _attn_kernel_mh,
        grid=(groups, bsz, seq // t),
        in_specs=[smem, smem, vec(HEAD_DK), vec(HEAD_DK), vec(HEAD_DK), vec(HEAD_DK),
                  vec(HEAD_DV),
                  pl.BlockSpec((None, t, width), lambda g, b, i: (b, i, g)),
                  pl.BlockSpec((None, seq, width), lambda g, b, i: (b, 0, groups + g)),
                  pl.BlockSpec((None, seq, width), lambda g, b, i: (b, 0, 2 * groups + g)),
                  pl.BlockSpec((None, t, width), lambda g, b, i: (b, i, 3 * groups + g))],
        out_specs=pl.BlockSpec((None, t, width), lambda g, b, i: (b, i, g)),
        out_shape=jax.ShapeDtypeStruct((bsz, seq, D_ATT), BF16),
        scratch_shapes=[pltpu.VMEM((nh, t, 2 * t), F32),
                        pltpu.VMEM((nh, 2, t, LANES), F32),
                        pltpu.VMEM((nh, 2, t, 2 * HEAD_DV), F32),
                        pltpu.VMEM((nh, seq, 2 * HEAD_DV), BF16),
                        pltpu.VMEM((nh, 2, t, HEAD_DV), BF16),
                        pltpu.VMEM((nh, 2, t, t), F32),
                        pltpu.VMEM((nh, 2, t, t), F32),
                        pltpu.VMEM((nh, 2, t, LANES), F32),
                        pltpu.VMEM((nh, 2, t, LANES), F32)],
        compiler_params=pltpu.CompilerParams(
            dimension_semantics=("arbitrary", "arbitrary", "arbitrary"),
            vmem_limit_bytes=VMEM_LIMIT),
        name="diff_attention",
    )(scal, rel_bias, lq1, lk1, lq2, lk2, subln_g, proj, proj, proj, proj)


def _attn_kernel_t(scal_ref, relb_ref, lq1_ref, lk1_ref, lq2_ref, lk2_ref, sg_ref,
                   q_ref, k_ref, v_ref, z_ref, o_ref,
                   bias_ref, m_ref, acc_ref, vt_ref, qm_ref, sa_ref, sb_ref, mxa_ref, mxb_ref):
    t = ATT_T
    n_lane_tiles = t // LANES
    heads = range(ATT_HEADS)
    hp = pl.program_id(0)
    b = pl.program_id(1)
    qi = pl.program_id(2)
    lanes_of = lambda hh: slice(hh * HEAD_DV, (hh + 1) * HEAD_DV)

    @pl.when((b == 0) & (qi == 0))
    def _():
        w_keys, w_queries = 3 * LANES, LANES
        key = lax.broadcasted_iota(jnp.int32, (w_keys, w_queries), 0) - LANES
        query = lax.broadcasted_iota(jnp.int32, (w_keys, w_queries), 1)
        rel = key - query
        half = N_BUCKETS // 2
        max_exact = half // 2
        n = jnp.abs(rel)
        nf = jnp.maximum(n, 1).astype(F32)
        large = max_exact + (jnp.log(nf / max_exact) / math.log(MAX_DISTANCE / max_exact)
                             * (half - max_exact)).astype(jnp.int32)
        large = jnp.minimum(large, half - 1)
        bucket = jnp.where(rel > 0, half, 0) + jnp.where(n < max_exact, n, large)
        allowed = ((key + LANES) // CHUNK - LANES // CHUNK) <= (query // CHUNK)
        for hh in heads:
            h = hp * ATT_HEADS + hh
            bias = jnp.zeros((w_keys, w_queries), F32)
            for i in range(N_BUCKETS):
                bias = jnp.where(bucket == i, relb_ref[i, h], bias)
            far = relb_ref[half - 1, h]
            window = jnp.where(allowed, (bias - far) * LOG2E, NEG_INF)
            for i in range(n_lane_tiles):
                cols = slice(i * LANES, (i + 1) * LANES)
                lo = (n_lane_tiles + i - 1) * LANES
                hi = min(lo + w_keys, 2 * t)
                bias_ref[hh, :lo, cols] = jnp.zeros((lo, LANES), F32)
                bias_ref[hh, lo:hi, cols] = window[:hi - lo, :]
                if hi < 2 * t:
                    bias_ref[hh, hi:, cols] = jnp.full((2 * t - hi, LANES), NEG_INF, F32)

    @pl.when(qi == 0)
    def _():
        for hh in heads:
            for j in range(v_ref.shape[0] // t):
                vj = v_ref[j * t:(j + 1) * t, lanes_of(hh)].astype(F32)
                vt_ref[hh, j, :HEAD_DV, :] = vj.T.astype(BF16)
                vt_ref[hh, j, HEAD_DV:, :] = jnp.ones((VT_ROWS - HEAD_DV, t), BF16)

    lane = lax.broadcasted_iota(jnp.int32, (t, HEAD_DV), 1)
    for hh in heads:
        q = q_ref[:, lanes_of(hh)]
        zero = jnp.zeros_like(q)
        qm_ref[hh, 0] = jnp.where(lane < HEAD_DK, q, zero)
        qm_ref[hh, 1] = jnp.where(lane >= HEAD_DK, q, zero)

    def qk_scores(hh, s_ref, mx_ref, tile, diagonal=False):
        kj = k_ref[pl.ds(pl.multiple_of(tile * t, t), t), lanes_of(hh)]
        if diagonal:
            bias = bias_ref[hh, t:, :]
        else:
            corner = jnp.where(tile == qi - 1, bias_ref[hh, t - LANES:t, :LANES], 0.0)
        for mi in range(2):
            s = lax.dot_general(kj, qm_ref[hh, mi], (((1,), (1,)), ((), ())),
                                preferred_element_type=F32)
            if diagonal:
                s = s + bias
            else:
                low = jnp.concatenate([s[t - LANES:, :LANES] + corner, s[t - LANES:, LANES:]],
                                      axis=1)
                s = jnp.concatenate([s[:t - LANES], low], axis=0)
            s_ref[hh, mi] = s
            mx_ref[hh, mi] = jnp.broadcast_to(jnp.max(s, axis=0, keepdims=True), (SUBLANES, t))

    def first_tile(hh, s_ref, mx_ref, tile):
        vt = vt_ref[hh, tile]
        for mi in range(2):
            m_new = mx_ref[hh, mi]
            p = jnp.exp2(s_ref[hh, mi] - m_new[:1])
            m_ref[hh, mi] = m_new
            acc_ref[hh, mi] = jnp.dot(vt, p.astype(BF16), preferred_element_type=F32)

    def next_tile(hh, s_ref, mx_ref, tile):
        vt = vt_ref[hh, tile]
        for mi in range(2):
            m_prev = m_ref[hh, mi]
            m_new = jnp.maximum(m_prev, mx_ref[hh, mi])
            alpha = jnp.exp2(m_prev - m_new)
            p = jnp.exp2(s_ref[hh, mi] - m_new[:1])
            m_ref[hh, mi] = m_new
            acc_ref[hh, mi] = (alpha[:1] * acc_ref[hh, mi]
                               + jnp.dot(vt, p.astype(BF16), preferred_element_type=F32))

    lam_init = scal_ref[0]
    lam = (jnp.exp(jnp.sum(lq1_ref[...] * lk1_ref[...], axis=1, keepdims=True))
           - jnp.exp(jnp.sum(lq2_ref[...] * lk2_ref[...], axis=1, keepdims=True))
           + lam_init)

    n_prev = qi
    last = jnp.maximum(n_prev - 1, 0)
    for hh in heads:
        qk_scores(hh, sa_ref, mxa_ref, qi, diagonal=True)
        qk_scores(hh, sb_ref, mxb_ref, 0)
        first_tile(hh, sa_ref, mxa_ref, qi)

        def pair_body(i, carry, hh=hh):
            qk_scores(hh, sa_ref, mxa_ref, jnp.minimum(2 * i + 1, last))
            next_tile(hh, sb_ref, mxb_ref, 2 * i)
            qk_scores(hh, sb_ref, mxb_ref, jnp.minimum(2 * i + 2, last))
            next_tile(hh, sa_ref, mxa_ref, 2 * i + 1)
            return carry

        lax.fori_loop(0, n_prev // 2, pair_body, 0)

        @pl.when(n_prev % 2 == 1)
        def _(hh=hh):
            next_tile(hh, sb_ref, mxb_ref, last)

        o_t = (acc_ref[hh, 0, :HEAD_DV, :] / acc_ref[hh, 0, HEAD_DV:HEAD_DV + 1, :]
               - lam * (acc_ref[hh, 1, :HEAD_DV, :] / acc_ref[hh, 1, HEAD_DV:HEAD_DV + 1, :]))
        o = o_t.T
        o = o * lax.rsqrt(jnp.mean(o * o, axis=-1, keepdims=True) + EPS) * sg_ref[...]
        o = o * (1.0 - lam_init)
        z = z_ref[:, lanes_of(hh)].astype(F32)
        o_ref[:, lanes_of(hh)] = (o * (z * jax.nn.sigmoid(z))).astype(o_ref.dtype)


def _attention_t(proj, scal, rel_bias, lq1, lk1, lq2, lk2, subln_g):
    bsz, seq, _ = proj.shape
    t = ATT_T
    nh = ATT_HEADS
    width = nh * HEAD_DV
    groups = N_HEADS // nh
    smem = pl.BlockSpec(memory_space=pltpu.SMEM)
    vec = lambda n: pl.BlockSpec((1, n), lambda g, b, i: (0, 0))
    return pl.pallas_call(
        _attn_kernel_t,
        grid=(groups, bsz, seq // t),
        in_specs=[smem, smem, vec(HEAD_DK), vec(HEAD_DK), vec(HEAD_DK), vec(HEAD_DK),
                  vec(HEAD_DV),
                  pl.BlockSpec((None, t, width), lambda g, b, i: (b, i, g)),
                  pl.BlockSpec((None, seq, width), lambda g, b, i: (b, 0, groups + g)),
                  pl.BlockSpec((None, seq, width), lambda g, b, i: (b, 0, 2 * groups + g)),
                  pl.BlockSpec((None, t, width), lambda g, b, i: (b, i, 3 * groups + g))],
        out_specs=pl.BlockSpec((None, t, width), lambda g, b, i: (b, i, g)),
        out_shape=jax.ShapeDtypeStruct((bsz, seq, D_ATT), BF16),
        scratch_shapes=[pltpu.VMEM((nh, 2 * t, t), F32),
                        pltpu.VMEM((nh, 2, SUBLANES, t), F32),
                        pltpu.VMEM((nh, 2, VT_ROWS, t), F32),
                        pltpu.VMEM((nh, seq // t, VT_ROWS, t), BF16),
                        pltpu.VMEM((nh, 2, t, HEAD_DV), BF16),
                        pltpu.VMEM((nh, 2, t, t), F32),
                        pltpu.VMEM((nh, 2, t, t), F32),
                        pltpu.VMEM((nh, 2, SUBLANES, t), F32),
                        pltpu.VMEM((nh, 2, SUBLANES, t), F32)],
        compiler_params=pltpu.CompilerParams(
            dimension_semantics=("arbitrary", "arbitrary", "arbitrary"),
            vmem_limit_bytes=VMEM_LIMIT),
        name="diff_attention",
    )(scal, rel_bias, lq1, lk1, lq2, lk2, subln_g, proj, proj, proj, proj)


def _gelu_tanh(x):
    return 0.5 * x * (1.0 + jnp.tanh(math.sqrt(2.0 / math.pi) * (x + 0.044715 * (x * x * x))))


def _ssm_kernel(u_ref, z_ref, bw_ref, cw_ref, are_ref, aim_ref, d_ref, wglu_ref,
                o_ref, hre_ref, him_ref, st_ref, g_ref):
    nb = u_ref.shape[0]
    tb = SSM_TB
    pitch = SSM_PITCH
    pairs_per_chunk = MXU_DIM // (2 * SSM_GROUP)
    n_vregs = N_PAIRS // SUBLANES

    def pair_rows(b, gp):
        return slice((b * N_PAIRS + gp) * pitch, (b * N_PAIRS + gp) * pitch + tb)

    @pl.when(pl.program_id(0) == 0)
    def _():
        st_ref[...] = jnp.zeros_like(st_ref)

    for gp in range(N_PAIRS):
        kc = gp // pairs_per_chunk
        uc = jnp.concatenate([u_ref[b, :, kc * MXU_DIM:(kc + 1) * MXU_DIM] for b in range(nb)],
                             axis=0)
        bu = jnp.dot(uc, bw_ref[gp], preferred_element_type=F32)
        for b in range(nb):
            hre_ref[pair_rows(b, gp), :] = bu[b * tb:(b + 1) * tb, :LANES]
            him_ref[pair_rows(b, gp), :] = bu[b * tb:(b + 1) * tb, LANES:]

    a_re = [are_ref[k] for k in range(n_vregs)]
    a_im = [aim_ref[k] for k in range(n_vregs)]

    def step(ti, carry):
        out = []
        for b in range(nb):
            for k in range(n_vregs):
                h_re, h_im = carry[2 * (b * n_vregs + k)], carry[2 * (b * n_vregs + k) + 1]
                rows = pl.ds((b * N_PAIRS + k * SUBLANES) * pitch + ti, SUBLANES, stride=pitch)
                n_re = a_re[k] * h_re - a_im[k] * h_im + hre_ref[rows, :]
                n_im = a_re[k] * h_im + a_im[k] * h_re + him_ref[rows, :]
                hre_ref[rows, :] = n_re
                him_ref[rows, :] = n_im
                out += [n_re, n_im]
        return tuple(out)

    init = tuple(st_ref[i] for i in range(2 * nb * n_vregs))
    final = lax.fori_loop(0, tb, step, init, unroll=8)
    for i, s in enumerate(final):
        st_ref[i] = s

    for kc in range(D_SSM // MXU_DIM):
        gps = range(kc * pairs_per_chunk, (kc + 1) * pairs_per_chunk)
        hcat = jnp.concatenate(
            [jnp.concatenate([ref[pair_rows(b, gp), :]
                              for gp in gps for ref in (hre_ref, him_ref)], axis=1)
             for b in range(nb)], axis=0).astype(BF16)
        cwk = cw_ref[kc * pairs_per_chunk:(kc + 1) * pairs_per_chunk]
        acc = jnp.dot(hcat, cwk.reshape(pairs_per_chunk * MXU_DIM, MXU_DIM),
                      preferred_element_type=F32)
        cols = slice(kc * MXU_DIM, (kc + 1) * MXU_DIM)
        for b in range(nb):
            y = acc[b * tb:(b + 1) * tb] + d_ref[:, cols] * u_ref[b, :, cols].astype(F32)
            g_ref[b * tb:(b + 1) * tb, cols] = _gelu_tanh(y).astype(BF16)

    glu = jnp.dot(g_ref[...], wglu_ref[...], preferred_element_type=F32)
    for b in range(nb):
        gb = glu[b * tb:(b + 1) * tb]
        z = z_ref[b].astype(F32)
        o = gb[:, :D_SSM] * jax.nn.sigmoid(gb[:, D_SSM:]) * (z * jax.nn.sigmoid(z))
        o_ref[b] = o.astype(o_ref.dtype)


def _ssm(proj, bw, cw, a_re_v, a_im_v, d_skip, w_glu):
    bsz, seq, _ = proj.shape
    tb = SSM_TB
    const = lambda shape: pl.BlockSpec(shape, lambda i: (0,) * len(shape),
                                       pipeline_mode=pl.Buffered(1))
    u_col = 4 * D_ATT // D_SSM
    n_state_vregs = 2 * bsz * (N_PAIRS // SUBLANES)
    return pl.pallas_call(
        _ssm_kernel,
        grid=(seq // tb,),
        in_specs=[pl.BlockSpec((bsz, tb, D_SSM), lambda i: (0, i, u_col)),
                  pl.BlockSpec((bsz, tb, D_SSM), lambda i: (0, i, u_col + 1)),
                  const((N_PAIRS, MXU_DIM, MXU_DIM)),
                  const((N_PAIRS, MXU_DIM, MXU_DIM)),
                  const((N_PAIRS // SUBLANES, SUBLANES, LANES)),
                  const((N_PAIRS // SUBLANES, SUBLANES, LANES)),
                  const((1, D_SSM)),
                  const((D_SSM, 2 * D_SSM))],
        out_specs=pl.BlockSpec((bsz, tb, D_SSM), lambda i: (0, i, 0)),
        out_shape=jax.ShapeDtypeStruct((bsz, seq, D_SSM), BF16),
        scratch_shapes=[pltpu.VMEM((bsz * N_PAIRS * SSM_PITCH, LANES), F32),
                        pltpu.VMEM((bsz * N_PAIRS * SSM_PITCH, LANES), F32),
                        pltpu.VMEM((n_state_vregs, SUBLANES, LANES), F32),
                        pltpu.VMEM((bsz * tb, D_SSM), BF16)],
        compiler_params=pltpu.CompilerParams(
            dimension_semantics=("arbitrary",),
            vmem_limit_bytes=VMEM_LIMIT),
        name="s5_glu",
    )(proj, proj, bw, cw, a_re_v, a_im_v, d_skip, w_glu)


def _ssm_weights(ab_re, ab_im, bb_re, bb_im, c_re, c_im):
    ppc = MXU_DIM // (2 * SSM_GROUP)
    sel = (jnp.arange(N_PAIRS)[:, None] % ppc == jnp.arange(ppc)[None, :]).astype(F32)
    group = jnp.arange(2)[None, :, None, None]

    def place(x_re, x_im):
        pieces = []
        for x in (x_re, x_im):
            xg = x.reshape(N_PAIRS, 2, SSM_GROUP, STATE)
            for gl2 in range(2):
                piece = jnp.where(group == gl2, xg, 0.0)[:, None] * sel[:, :, None, None, None]
                pieces.append(piece.reshape(N_PAIRS, ppc, 2 * SSM_GROUP, STATE))
        w = jnp.concatenate(pieces, axis=-1)
        return w.reshape(N_PAIRS, MXU_DIM, MXU_DIM)

    bw = place(bb_re, bb_im).astype(BF16)
    cw = jnp.swapaxes(place(c_re, -c_im), 1, 2).astype(BF16)
    vshape = (N_PAIRS // SUBLANES, SUBLANES, LANES)
    return bw, cw, ab_re.reshape(vshape), ab_im.reshape(vshape)


def _out_proj_kernel(oa_ref, os_ref, w_ref, g_ref, x_ref, o_ref):
    mix = (jnp.dot(oa_ref[...], w_ref[:D_ATT, :], preferred_element_type=F32)
           + jnp.dot(os_ref[...], w_ref[D_ATT:, :], preferred_element_type=F32))
    y = mix * lax.rsqrt(jnp.mean(mix * mix, axis=-1, keepdims=True) + EPS)
    o_ref[...] = x_ref[...] + y * g_ref[...]


def _out_proj(o_att, o_ssm, w, g, x2d):
    m = x2d.shape[0]
    return pl.pallas_call(
        _out_proj_kernel,
        grid=(m // OUT_TM,),
        in_specs=[pl.BlockSpec((OUT_TM, D_ATT), lambda i: (i, 0)),
                  pl.BlockSpec((OUT_TM, D_SSM), lambda i: (i, 0)),
                  pl.BlockSpec((D_MODEL, D_MODEL), lambda i: (0, 0),
                               pipeline_mode=pl.Buffered(1)),
                  pl.BlockSpec((1, D_MODEL), lambda i: (0, 0)),
                  pl.BlockSpec((OUT_TM, D_MODEL), lambda i: (i, 0))],
        out_specs=pl.BlockSpec((OUT_TM, D_MODEL), lambda i: (i, 0)),
        out_shape=jax.ShapeDtypeStruct((m, D_MODEL), F32),
        compiler_params=pltpu.CompilerParams(
            dimension_semantics=("arbitrary",),
            vmem_limit_bytes=VMEM_LIMIT),
        name="out_proj",
    )(o_att, o_ssm, w, g, x2d)


def kernel(x, rel_bias, pre_norm_g, post_norm_g, w_in, lambda_q1, lambda_k1, lambda_q2,
           lambda_k2, subln_g, ssm_a_re, ssm_a_im, ssm_log_dt, ssm_b_re, ssm_b_im,
           ssm_c_re, ssm_c_im, ssm_d, w_glu, w_out):
    bsz, seq, _ = x.shape
    m = bsz * seq
    ab_re, ab_im, bb_re, bb_im = _discretize(ssm_a_re, ssm_a_im, ssm_log_dt, ssm_b_re, ssm_b_im)
    col_scale = jnp.concatenate([jnp.full((D_ATT,), HEAD_DK ** -0.5 * LOG2E, F32),
                                 jnp.ones((D_IN - D_ATT,), F32)])
    x2d = x.reshape(m, D_MODEL)
    for l in range(DEPTH):
        lam_init = _lambda_init(l)
        w_in_l = (w_in[l] * col_scale).astype(BF16)
        proj = _in_proj(x2d, pre_norm_g[l].reshape(1, D_MODEL), w_in_l)
        proj = proj.reshape(bsz, seq, D_IN)
        scal = jnp.array([lam_init], F32)
        o_att = _attention_t(proj, scal, rel_bias,
                           lambda_q1[l].reshape(1, HEAD_DK), lambda_k1[l].reshape(1, HEAD_DK),
                           lambda_q2[l].reshape(1, HEAD_DK), lambda_k2[l].reshape(1, HEAD_DK),
                           subln_g[l].reshape(1, HEAD_DV))
        bw, cw, a_re_v, a_im_v = _ssm_weights(
            ab_re[l, :, 0, :], ab_im[l, :, 0, :], bb_re[l], bb_im[l], ssm_c_re[l], ssm_c_im[l])
        o_ssm = _ssm(proj, bw, cw, a_re_v, a_im_v, ssm_d[l].reshape(1, D_SSM),
                     w_glu[l].astype(BF16))
        x2d = _out_proj(o_att.reshape(m, D_ATT), o_ssm.reshape(m, D_SSM),
                        w_out[l].astype(BF16), post_norm_g[l].reshape(1, D_MODEL), x2d)
    return x2d.reshape(bsz, seq, D_MODEL)
```

```python
import math

import jax
import jax.numpy as jnp
from jax import lax
from jax.experimental import pallas as pl
from jax.experimental.pallas import tpu as pltpu

F32 = jnp.float32
BF16 = jnp.bfloat16

D_MODEL = 2048
DEPTH = 4
CHUNK = 64
D_ATT = 1024
D_SSM = 1024
N_HEADS = 8
HEAD_DV = 128
HEAD_DK = 64
SSM_GROUP = 16
N_GROUPS = 64
STATE = 64
N_BUCKETS = 32
MAX_DISTANCE = 128
EPS = 1e-6
NEG_INF = -1e30
LOG2E = math.log2(math.e)
D_IN = 4 * D_ATT + 2 * D_SSM

LANES = 128
SUBLANES = 8
BF16_ROWS = 16
MXU_DIM = 256
VMEM_LIMIT = 56 * 1024 * 1024

PROJ_TM = 1024
PROJ_TN = 1024
ATT_T = 512
ATT_HEADS = 2
VT_ROWS = HEAD_DV + BF16_ROWS
SSM_TB = 256
SSM_PITCH = SSM_TB + SUBLANES // 2
N_PAIRS = N_GROUPS // 2
OUT_TM = 512


def _lambda_init(layer_idx):
    return 0.8 - 0.6 * math.exp(-0.3 * layer_idx)


def _discretize_kernel(a_re_ref, a_im_ref, log_dt_ref, b_re_ref, b_im_ref,
                       ab_re_ref, ab_im_ref, bb_re_ref, bb_im_ref):
    a_re = a_re_ref[...]
    a_im = a_im_ref[...]
    dt = jnp.exp(log_dt_ref[...])
    mag = jnp.exp(dt * a_re)
    ab_re = mag * jnp.cos(dt * a_im)
    ab_im = mag * jnp.sin(dt * a_im)
    den = a_re * a_re + a_im * a_im
    nr = ab_re - 1.0
    cf_re = (nr * a_re + ab_im * a_im) / den
    cf_im = (ab_im * a_re - nr * a_im) / den
    b_re = b_re_ref[...]
    b_im = b_im_ref[...]
    ab_re_ref[...] = ab_re
    ab_im_ref[...] = ab_im
    bb_re_ref[...] = cf_re * b_re - cf_im * b_im
    bb_im_ref[...] = cf_re * b_im + cf_im * b_re


def _discretize(a_re, a_im, log_dt, b_re, b_im):
    nl = a_re.shape[0]
    a4 = (nl, N_GROUPS, 1, STATE)
    b4 = (nl, N_GROUPS, SSM_GROUP, STATE)
    return pl.pallas_call(
        _discretize_kernel,
        out_shape=(jax.ShapeDtypeStruct(a4, F32), jax.ShapeDtypeStruct(a4, F32),
                   jax.ShapeDtypeStruct(b4, F32), jax.ShapeDtypeStruct(b4, F32)),
        name="s5_discretize",
    )(a_re.reshape(a4), a_im.reshape(a4), log_dt.reshape(nl, N_GROUPS, 1, 1),
      jnp.swapaxes(b_re, 2, 3), jnp.swapaxes(b_im, 2, 3))


def _in_proj_kernel(x_ref, g_ref, w_ref, o_ref, h_ref):
    @pl.when(pl.program_id(1) == 0)
    def _():
        x = x_ref[...]
        y = x * lax.rsqrt(jnp.mean(x * x, axis=-1, keepdims=True) + EPS)
        h_ref[...] = (y * g_ref[...]).astype(BF16)

    o_ref[...] = jnp.dot(h_ref[...], w_ref[...],
                         preferred_element_type=F32).astype(o_ref.dtype)


def _in_proj(x2d, g, w):
    m = x2d.shape[0]
    return pl.pallas_call(
        _in_proj_kernel,
        grid=(m // PROJ_TM, D_IN // PROJ_TN),
        in_specs=[pl.BlockSpec((PROJ_TM, D_MODEL), lambda i, j: (i, 0)),
                  pl.BlockSpec((1, D_MODEL), lambda i, j: (0, 0)),
                  pl.BlockSpec((D_MODEL, PROJ_TN), lambda i, j: (0, j))],
        out_specs=pl.BlockSpec((PROJ_TM, PROJ_TN), lambda i, j: (i, j)),
        out_shape=jax.ShapeDtypeStruct((m, D_IN), BF16),
        scratch_shapes=[pltpu.VMEM((PROJ_TM, D_MODEL), BF16)],
        compiler_params=pltpu.CompilerParams(
            dimension_semantics=("arbitrary", "arbitrary"),
            vmem_limit_bytes=VMEM_LIMIT),
        name="in_proj",
    )(x2d, g, w)


def _attn_kernel(scal_ref, relb_ref, lq1_ref, lk1_ref, lq2_ref, lk2_ref, sg_ref,
                 q_ref, k_ref, v_ref, z_ref, o_ref,
                 bias_ref, m_ref, acc_ref, vt_ref, qm_ref, sa_ref, sb_ref, mxa_ref, mxb_ref):
    t = ATT_T
    n_lane_tiles = t // LANES
    heads = range(ATT_HEADS)
    hp = pl.program_id(0)
    b = pl.program_id(1)
    qi = pl.program_id(2)
    lanes_of = lambda hh: slice(hh * HEAD_DV, (hh + 1) * HEAD_DV)

    @pl.when((b == 0) & (qi == 0))
    def _():
        w_keys, w_queries = 3 * LANES, LANES
        key = lax.broadcasted_iota(jnp.int32, (w_keys, w_queries), 0) - LANES
        query = lax.broadcasted_iota(jnp.int32, (w_keys, w_queries), 1)
        rel = key - query
        half = N_BUCKETS // 2
        max_exact = half // 2
        n = jnp.abs(rel)
        nf = jnp.maximum(n, 1).astype(F32)
        large = max_exact + (jnp.log(nf / max_exact) / math.log(MAX_DISTANCE / max_exact)
                             * (half - max_exact)).astype(jnp.int32)
        large = jnp.minimum(large, half - 1)
        bucket = jnp.where(rel > 0, half, 0) + jnp.where(n < max_exact, n, large)
        allowed = ((key + LANES) // CHUNK - LANES // CHUNK) <= (query // CHUNK)
        for hh in heads:
            h = hp * ATT_HEADS + hh
            bias = jnp.zeros((w_keys, w_queries), F32)
            for i in range(N_BUCKETS):
                bias = jnp.where(bucket == i, relb_ref[i, h], bias)
            far = relb_ref[half - 1, h]
            window = jnp.where(allowed, (bias - far) * LOG2E, NEG_INF)
            for i in range(n_lane_tiles):
                cols = slice(i * LANES, (i + 1) * LANES)
                lo = (n_lane_tiles + i - 1) * LANES
                hi = min(lo + w_keys, 2 * t)
                bias_ref[hh, :lo, cols] = jnp.zeros((lo, LANES), F32)
                bias_ref[hh, lo:hi, cols] = window[:hi - lo, :]
                if hi < 2 * t:
                    bias_ref[hh, hi:, cols] = jnp.full((2 * t - hi, LANES), NEG_INF, F32)

    @pl.when(qi == 0)
    def _():
        for hh in heads:
            for j in range(v_ref.shape[0] // t):
                vj = v_ref[j * t:(j + 1) * t, lanes_of(hh)].astype(F32)
                vt_ref[hh, j, :HEAD_DV, :] = vj.T.astype(BF16)
                vt_ref[hh, j, HEAD_DV:, :] = jnp.ones((VT_ROWS - HEAD_DV, t), BF16)

    lane = lax.broadcasted_iota(jnp.int32, (t, HEAD_DV), 1)
    for hh in heads:
        q = q_ref[:, lanes_of(hh)]
        zero = jnp.zeros_like(q)
        qm_ref[hh, 0] = jnp.where(lane < HEAD_DK, q, zero)
        qm_ref[hh, 1] = jnp.where(lane >= HEAD_DK, q, zero)

    def qk_scores(hh, s_ref, mx_ref, tile, diagonal=False):
        kj = k_ref[pl.ds(pl.multiple_of(tile * t, t), t), lanes_of(hh)]
        if diagonal:
            bias = bias_ref[hh, t:, :]
        else:
            corner = jnp.where(tile == qi - 1, bias_ref[hh, t - LANES:t, :LANES], 0.0)
        for mi in range(2):
            s = lax.dot_general(kj, qm_ref[hh, mi], (((1,), (1,)), ((), ())),
                                preferred_element_type=F32)
            if diagonal:
                s = s + bias
            else:
                low = jnp.concatenate([s[t - LANES:, :LANES] + corner, s[t - LANES:, LANES:]],
                                      axis=1)
                s = jnp.concatenate([s[:t - LANES], low], axis=0)
            s_ref[hh, mi] = s
            mx_ref[hh, mi] = jnp.broadcast_to(jnp.max(s, axis=0, keepdims=True), (SUBLANES, t))

    def first_tile(hh, s_ref, mx_ref, tile):
        vt = vt_ref[hh, tile]
        for mi in range(2):
            m_new = mx_ref[hh, mi]
            p = jnp.exp2(s_ref[hh, mi] - m_new[:1])
            m_ref[hh, mi] = m_new
            acc_ref[hh, mi] = jnp.dot(vt, p.astype(BF16), preferred_element_type=F32)

    def next_tile(hh, s_ref, mx_ref, tile):
        vt = vt_ref[hh, tile]
        for mi in range(2):
            m_prev = m_ref[hh, mi]
            m_new = jnp.maximum(m_prev, mx_ref[hh, mi])
            alpha = jnp.exp2(m_prev - m_new)
            p = jnp.exp2(s_ref[hh, mi] - m_new[:1])
            m_ref[hh, mi] = m_new
            acc_ref[hh, mi] = (alpha[:1] * acc_ref[hh, mi]
                               + jnp.dot(vt, p.astype(BF16), preferred_element_type=F32))

    lam_init = scal_ref[0]
    lam = (jnp.exp(jnp.sum(lq1_ref[...] * lk1_ref[...], axis=1, keepdims=True))
           - jnp.exp(jnp.sum(lq2_ref[...] * lk2_ref[...], axis=1, keepdims=True))
           + lam_init)

    n_prev = qi
    last = jnp.maximum(n_prev - 1, 0)
    for hh in heads:
        qk_scores(hh, sa_ref, mxa_ref, qi, diagonal=True)
        qk_scores(hh, sb_ref, mxb_ref, 0)
        first_tile(hh, sa_ref, mxa_ref, qi)

        def two_tiles(first, hh=hh):
            qk_scores(hh, sa_ref, mxa_ref, jnp.minimum(first + 1, last))
            next_tile(hh, sb_ref, mxb_ref, first)
            qk_scores(hh, sb_ref, mxb_ref, jnp.minimum(first + 2, last))
            next_tile(hh, sa_ref, mxa_ref, first + 1)

        def quad_body(i, carry):
            two_tiles(4 * i)
            two_tiles(4 * i + 2)
            return carry

        n_quads = n_prev // 4
        lax.fori_loop(0, n_quads, quad_body, 0)

        def pair_body(i, carry):
            two_tiles(4 * n_quads + 2 * i)
            return carry

        lax.fori_loop(0, (n_prev % 4) // 2, pair_body, 0)

        @pl.when(n_prev % 2 == 1)
        def _(hh=hh):
            next_tile(hh, sb_ref, mxb_ref, last)

        o_t = (acc_ref[hh, 0, :HEAD_DV, :] / acc_ref[hh, 0, HEAD_DV:HEAD_DV + 1, :]
               - lam * (acc_ref[hh, 1, :HEAD_DV, :] / acc_ref[hh, 1, HEAD_DV:HEAD_DV + 1, :]))
        o = o_t.T
        o = o * lax.rsqrt(jnp.mean(o * o, axis=-1, keepdims=True) + EPS) * sg_ref[...]
        o = o * (1.0 - lam_init)
        z = z_ref[:, lanes_of(hh)].astype(F32)
        o_ref[:, lanes_of(hh)] = (o * (z * jax.nn.sigmoid(z))).astype(o_ref.dtype)


def _attention(proj, scal, rel_bias, lq1, lk1, lq2, lk2, subln_g):
    bsz, seq, _ = proj.shape
    t = ATT_T
    nh = ATT_HEADS
    width = nh * HEAD_DV
    groups = N_HEADS // nh
    smem = pl.BlockSpec(memory_space=pltpu.SMEM)
    vec = lambda n: pl.BlockSpec((1, n), lambda g, b, i: (0, 0))
    return pl.pallas_call(
        _attn_kernel,
        grid=(groups, bsz, seq // t),
        in_specs=[smem, smem, vec(HEAD_DK), vec(HEAD_DK), vec(HEAD_DK), vec(HEAD_DK),
                  vec(HEAD_DV),
                  pl.BlockSpec((None, t, width), lambda g, b, i: (b, i, g)),
                  pl.BlockSpec((None, seq, width), lambda g, b, i: (b, 0, groups + g)),
                  pl.BlockSpec((None, seq, width), lambda g, b, i: (b, 0, 2 * groups + g)),
                  pl.BlockSpec((None, t, width), lambda g, b, i: (b, i, 3 * groups + g))],
        out_specs=pl.BlockSpec((None, t, width), lambda g, b, i: (b, i, g)),
        out_shape=jax.ShapeDtypeStruct((bsz, seq, D_ATT), BF16),
        scratch_shapes=[pltpu.VMEM((nh, 2 * t, t), F32),
                        pltpu.VMEM((nh, 2, SUBLANES, t), F32),
                        pltpu.VMEM((nh, 2, VT_ROWS, t), F32),
                        pltpu.VMEM((nh, seq // t, VT_ROWS, t), BF16),
                        pltpu.VMEM((nh, 2, t, HEAD_DV), BF16),
                        pltpu.VMEM((nh, 2, t, t), F32),
                        pltpu.VMEM((nh, 2, t, t), F32),
                        pltpu.VMEM((nh, 2, SUBLANES, t), F32),
                        pltpu.VMEM((nh, 2, SUBLANES, t), F32)],
        compiler_params=pltpu.CompilerParams(
            dimension_semantics=("arbitrary", "arbitrary", "arbitrary"),
            vmem_limit_bytes=VMEM_LIMIT),
        name="diff_attention",
    )(scal, rel_bias, lq1, lk1, lq2, lk2, subln_g, proj, proj, proj, proj)


def _gelu_tanh(x):
    return 0.5 * x * (1.0 + jnp.tanh(math.sqrt(2.0 / math.pi) * (x + 0.044715 * (x * x * x))))


def _ssm_kernel(u_ref, z_ref, bw_ref, cw_ref, are_ref, aim_ref, d_ref, wglu_ref,
                o_ref, hre_ref, him_ref, st_ref, g_ref):
    nb = u_ref.shape[0]
    tb = SSM_TB
    pitch = SSM_PITCH
    pairs_per_chunk = MXU_DIM // (2 * SSM_GROUP)
    n_vregs = N_PAIRS // SUBLANES

    def pair_rows(b, gp):
        return slice((b * N_PAIRS + gp) * pitch, (b * N_PAIRS + gp) * pitch + tb)

    @pl.when(pl.program_id(0) == 0)
    def _():
        st_ref[...] = jnp.zeros_like(st_ref)

    for gp in range(N_PAIRS):
        kc = gp // pairs_per_chunk
        uc = jnp.concatenate([u_ref[b, :, kc * MXU_DIM:(kc + 1) * MXU_DIM] for b in range(nb)],
                             axis=0)
        bu = jnp.dot(uc, bw_ref[gp], preferred_element_type=F32)
        for b in range(nb):
            hre_ref[pair_rows(b, gp), :] = bu[b * tb:(b + 1) * tb, :LANES]
            him_ref[pair_rows(b, gp), :] = bu[b * tb:(b + 1) * tb, LANES:]

    a_re = [are_ref[k] for k in range(n_vregs)]
    a_im = [aim_ref[k] for k in range(n_vregs)]

    def step(ti, carry):
        out = []
        for b in range(nb):
            for k in range(n_vregs):
                h_re, h_im = carry[2 * (b * n_vregs + k)], carry[2 * (b * n_vregs + k) + 1]
                rows = pl.ds((b * N_PAIRS + k * SUBLANES) * pitch + ti, SUBLANES, stride=pitch)
                n_re = a_re[k] * h_re - a_im[k] * h_im + hre_ref[rows, :]
                n_im = a_re[k] * h_im + a_im[k] * h_re + him_ref[rows, :]
                hre_ref[rows, :] = n_re
                him_ref[rows, :] = n_im
                out += [n_re, n_im]
        return tuple(out)

    init = tuple(st_ref[i] for i in range(2 * nb * n_vregs))
    final = lax.fori_loop(0, tb, step, init, unroll=8)
    for i, s in enumerate(final):
        st_ref[i] = s

    for kc in range(D_SSM // MXU_DIM):
        gps = range(kc * pairs_per_chunk, (kc + 1) * pairs_per_chunk)
        hcat = jnp.concatenate(
            [jnp.concatenate([ref[pair_rows(b, gp), :]
                              for gp in gps for ref in (hre_ref, him_ref)], axis=1)
             for b in range(nb)], axis=0).astype(BF16)
        cwk = cw_ref[kc * pairs_per_chunk:(kc + 1) * pairs_per_chunk]
        acc = jnp.dot(hcat, cwk.reshape(pairs_per_chunk * MXU_DIM, MXU_DIM),
                      preferred_element_type=F32)
        cols = slice(kc * MXU_DIM, (kc + 1) * MXU_DIM)
        for b in range(nb):
            y = acc[b * tb:(b + 1) * tb] + d_ref[:, cols] * u_ref[b, :, cols].astype(F32)
            g_ref[b * tb:(b + 1) * tb, cols] = _gelu_tanh(y).astype(BF16)

    glu = jnp.dot(g_ref[...], wglu_ref[...], preferred_element_type=F32)
    for b in range(nb):
        gb = glu[b * tb:(b + 1) * tb]
        z = z_ref[b].astype(F32)
        o = gb[:, :D_SSM] * jax.nn.sigmoid(gb[:, D_SSM:]) * (z * jax.nn.sigmoid(z))
        o_ref[b] = o.astype(o_ref.dtype)


def _ssm(proj, bw, cw, a_re_v, a_im_v, d_skip, w_glu):
    bsz, seq, _ = proj.shape
    tb = SSM_TB
    const = lambda shape: pl.BlockSpec(shape, lambda i: (0,) * len(shape),
                                       pipeline_mode=pl.Buffered(1))
    u_col = 4 * D_ATT // D_SSM
    n_state_vregs = 2 * bsz * (N_PAIRS // SUBLANES)
    return pl.pallas_call(
        _ssm_kernel,
        grid=(seq // tb,),
        in_specs=[pl.BlockSpec((bsz, tb, D_SSM), lambda i: (0, i, u_col)),
                  pl.BlockSpec((bsz, tb, D_SSM), lambda i: (0, i, u_col + 1)),
                  const((N_PAIRS, MXU_DIM, MXU_DIM)),
                  const((N_PAIRS, MXU_DIM, MXU_DIM)),
                  const((N_PAIRS // SUBLANES, SUBLANES, LANES)),
                  const((N_PAIRS // SUBLANES, SUBLANES, LANES)),
                  const((1, D_SSM)),
                  const((D_SSM, 2 * D_SSM))],
        out_specs=pl.BlockSpec((bsz, tb, D_SSM), lambda i: (0, i, 0)),
        out_shape=jax.ShapeDtypeStruct((bsz, seq, D_SSM), BF16),
        scratch_shapes=[pltpu.VMEM((bsz * N_PAIRS * SSM_PITCH, LANES), F32),
                        pltpu.VMEM((bsz * N_PAIRS * SSM_PITCH, LANES), F32),
                        pltpu.VMEM((n_state_vregs, SUBLANES, LANES), F32),
                        pltpu.VMEM((bsz * tb, D_SSM), BF16)],
        compiler_params=pltpu.CompilerParams(
            dimension_semantics=("arbitrary",),
            vmem_limit_bytes=VMEM_LIMIT),
        name="s5_glu",
    )(proj, proj, bw, cw, a_re_v, a_im_v, d_skip, w_glu)


def _ssm_weights(ab_re, ab_im, bb_re, bb_im, c_re, c_im):
    ppc = MXU_DIM // (2 * SSM_GROUP)
    sel = (jnp.arange(N_PAIRS)[:, None] % ppc == jnp.arange(ppc)[None, :]).astype(F32)
    group = jnp.arange(2)[None, :, None, None]

    def place(x_re, x_im):
        pieces = []
        for x in (x_re, x_im):
            xg = x.reshape(N_PAIRS, 2, SSM_GROUP, STATE)
            for gl2 in range(2):
                piece = jnp.where(group == gl2, xg, 0.0)[:, None] * sel[:, :, None, None, None]
                pieces.append(piece.reshape(N_PAIRS, ppc, 2 * SSM_GROUP, STATE))
        w = jnp.concatenate(pieces, axis=-1)
        return w.reshape(N_PAIRS, MXU_DIM, MXU_DIM)

    bw = place(bb_re, bb_im).astype(BF16)
    cw = jnp.swapaxes(place(c_re, -c_im), 1, 2).astype(BF16)
    vshape = (N_PAIRS // SUBLANES, SUBLANES, LANES)
    return bw, cw, ab_re.reshape(vshape), ab_im.reshape(vshape)


def _out_proj_kernel(oa_ref, os_ref, w_ref, g_ref, x_ref, o_ref):
    mix = (jnp.dot(oa_ref[...], w_ref[:D_ATT, :], preferred_element_type=F32)
           + jnp.dot(os_ref[...], w_ref[D_ATT:, :], preferred_element_type=F32))
    y = mix * lax.rsqrt(jnp.mean(mix * mix, axis=-1, keepdims=True) + EPS)
    o_ref[...] = x_ref[...] + y * g_ref[...]


def _out_proj(o_att, o_ssm, w, g, x2d):
    m = x2d.shape[0]
    return pl.pallas_call(
        _out_proj_kernel,
        grid=(m // OUT_TM,),
        in_specs=[pl.BlockSpec((OUT_TM, D_ATT), lambda i: (i, 0)),
                  pl.BlockSpec((OUT_TM, D_SSM), lambda i: (i, 0)),
                  pl.BlockSpec((D_MODEL, D_MODEL), lambda i: (0, 0),
                               pipeline_mode=pl.Buffered(1)),
                  pl.BlockSpec((1, D_MODEL), lambda i: (0, 0)),
                  pl.BlockSpec((OUT_TM, D_MODEL), lambda i: (i, 0))],
        out_specs=pl.BlockSpec((OUT_TM, D_MODEL), lambda i: (i, 0)),
        out_shape=jax.ShapeDtypeStruct((m, D_MODEL), F32),
        compiler_params=pltpu.CompilerParams(
            dimension_semantics=("arbitrary",),
            vmem_limit_bytes=VMEM_LIMIT),
        name="out_proj",
    )(o_att, o_ssm, w, g, x2d)


def kernel(x, rel_bias, pre_norm_g, post_norm_g, w_in, lambda_q1, lambda_k1, lambda_q2,
           lambda_k2, subln_g, ssm_a_re, ssm_a_im, ssm_log_dt, ssm_b_re, ssm_b_im,
           ssm_c_re, ssm_c_im, ssm_d, w_glu, w_out):
    bsz, seq, _ = x.shape
    m = bsz * seq
    ab_re, ab_im, bb_re, bb_im = _discretize(ssm_a_re, ssm_a_im, ssm_log_dt, ssm_b_re, ssm_b_im)
    col_scale = jnp.concatenate([jnp.full((D_ATT,), HEAD_DK ** -0.5 * LOG2E, F32),
                                 jnp.ones((D_IN - D_ATT,), F32)])
    x2d = x.reshape(m, D_MODEL)
    for l in range(DEPTH):
        lam_init = _lambda_init(l)
        w_in_l = (w_in[l] * col_scale).astype(BF16)
        proj = _in_proj(x2d, pre_norm_g[l].reshape(1, D_MODEL), w_in_l)
        proj = proj.reshape(bsz, seq, D_IN)
        scal = jnp.array([lam_init], F32)
        o_att = _attention(proj, scal, rel_bias,
                           lambda_q1[l].reshape(1, HEAD_DK), lambda_k1[l].reshape(1, HEAD_DK),
                           lambda_q2[l].reshape(1, HEAD_DK), lambda_k2[l].reshape(1, HEAD_DK),
                           subln_g[l].reshape(1, HEAD_DV))
        bw, cw, a_re_v, a_im_v = _ssm_weights(
            ab_re[l, :, 0, :], ab_im[l, :, 0, :], bb_re[l], bb_im[l], ssm_c_re[l], ssm_c_im[l])
        o_ssm = _ssm(proj, bw, cw, a_re_v, a_im_v, ssm_d[l].reshape(1, D_SSM),
                     w_glu[l].astype(BF16))
        x2d = _out_proj(o_att.reshape(m, D_ATT), o_ssm.reshape(m, D_SSM),
                        w_out[l].astype(BF16), post_norm_g[l].reshape(1, D_MODEL), x2d)
    return x2d.reshape(bsz, seq, D_MODEL)
```

```python
import math

import jax
import jax.numpy as jnp
from jax import lax
from jax.experimental import pallas as pl
from jax.experimental.pallas import tpu as pltpu

F32 = jnp.float32
BF16 = jnp.bfloat16

D_MODEL = 2048
DEPTH = 4
CHUNK = 64
D_ATT = 1024
D_SSM = 1024
N_HEADS = 8
HEAD_DV = 128
HEAD_DK = 64
SSM_GROUP = 16
N_GROUPS = 64
STATE = 64
N_BUCKETS = 32
MAX_DISTANCE = 128
EPS = 1e-6
NEG_INF = -1e30
LOG2E = math.log2(math.e)
D_IN = 4 * D_ATT + 2 * D_SSM

LANES = 128
SUBLANES = 8
BF16_ROWS = 16
MXU_DIM = 256
VMEM_LIMIT = 56 * 1024 * 1024

PROJ_TM = 1024
PROJ_TN = 1024
ATT_T = 512
ATT_HEADS = 2
VT_ROWS = HEAD_DV + BF16_ROWS
SSM_TB = 256
SSM_PITCH = SSM_TB + SUBLANES // 2
N_PAIRS = N_GROUPS // 2
OUT_TM = 512


def _lambda_init(layer_idx):
    return 0.8 - 0.6 * math.exp(-0.3 * layer_idx)


def _discretize_kernel(a_re_ref, a_im_ref, log_dt_ref, b_re_ref, b_im_ref,
                       ab_re_ref, ab_im_ref, bb_re_ref, bb_im_ref):
    a_re = a_re_ref[...]
    a_im = a_im_ref[...]
    dt = jnp.exp(log_dt_ref[...])
    mag = jnp.exp(dt * a_re)
    ab_re = mag * jnp.cos(dt * a_im)
    ab_im = mag * jnp.sin(dt * a_im)
    den = a_re * a_re + a_im * a_im
    nr = ab_re - 1.0
    cf_re = (nr * a_re + ab_im * a_im) / den
    cf_im = (ab_im * a_re - nr * a_im) / den
    b_re = b_re_ref[...]
    b_im = b_im_ref[...]
    ab_re_ref[...] = ab_re
    ab_im_ref[...] = ab_im
    bb_re_ref[...] = cf_re * b_re - cf_im * b_im
    bb_im_ref[...] = cf_re * b_im + cf_im * b_re


def _discretize(a_re, a_im, log_dt, b_re, b_im):
    nl = a_re.shape[0]
    a4 = (nl, N_GROUPS, 1, STATE)
    b4 = (nl, N_GROUPS, SSM_GROUP, STATE)
    return pl.pallas_call(
        _discretize_kernel,
        out_shape=(jax.ShapeDtypeStruct(a4, F32), jax.ShapeDtypeStruct(a4, F32),
                   jax.ShapeDtypeStruct(b4, F32), jax.ShapeDtypeStruct(b4, F32)),
        name="s5_discretize",
    )(a_re.reshape(a4), a_im.reshape(a4), log_dt.reshape(nl, N_GROUPS, 1, 1),
      jnp.swapaxes(b_re, 2, 3), jnp.swapaxes(b_im, 2, 3))


def _in_proj_kernel(x_ref, g_ref, w_ref, o_ref, h_ref):
    @pl.when(pl.program_id(1) == 0)
    def _():
        x = x_ref[...]
        y = x * lax.rsqrt(jnp.mean(x * x, axis=-1, keepdims=True) + EPS)
        h_ref[...] = (y * g_ref[...]).astype(BF16)

    o_ref[...] = jnp.dot(h_ref[...], w_ref[...],
                         preferred_element_type=F32).astype(o_ref.dtype)


def _in_proj(x2d, g, w):
    m = x2d.shape[0]
    return pl.pallas_call(
        _in_proj_kernel,
        grid=(m // PROJ_TM, D_IN // PROJ_TN),
        in_specs=[pl.BlockSpec((PROJ_TM, D_MODEL), lambda i, j: (i, 0)),
                  pl.BlockSpec((1, D_MODEL), lambda i, j: (0, 0)),
                  pl.BlockSpec((D_MODEL, PROJ_TN), lambda i, j: (0, j))],
        out_specs=pl.BlockSpec((PROJ_TM, PROJ_TN), lambda i, j: (i, j)),
        out_shape=jax.ShapeDtypeStruct((m, D_IN), BF16),
        scratch_shapes=[pltpu.VMEM((PROJ_TM, D_MODEL), BF16)],
        compiler_params=pltpu.CompilerParams(
            dimension_semantics=("arbitrary", "arbitrary"),
            vmem_limit_bytes=VMEM_LIMIT),
        name="in_proj",
    )(x2d, g, w)


def _attn_kernel(scal_ref, relb_ref, lq1_ref, lk1_ref, lq2_ref, lk2_ref, sg_ref,
                 q_ref, k_ref, v_ref, z_ref, o_ref,
                 bias_ref, m_ref, acc_ref, vt_ref, qm_ref, sa_ref, sb_ref, mxa_ref, mxb_ref):
    t = ATT_T
    n_lane_tiles = t // LANES
    heads = range(ATT_HEADS)
    hp = pl.program_id(0)
    b = pl.program_id(1)
    qi = pl.program_id(2)
    lanes_of = lambda hh: slice(hh * HEAD_DV, (hh + 1) * HEAD_DV)

    @pl.when((b == 0) & (qi == 0))
    def _():
        w_keys, w_queries = 3 * LANES, LANES
        key = lax.broadcasted_iota(jnp.int32, (w_keys, w_queries), 0) - LANES
        query = lax.broadcasted_iota(jnp.int32, (w_keys, w_queries), 1)
        rel = key - query
        half = N_BUCKETS // 2
        max_exact = half // 2
        n = jnp.abs(rel)
        nf = jnp.maximum(n, 1).astype(F32)
        large = max_exact + (jnp.log(nf / max_exact) / math.log(MAX_DISTANCE / max_exact)
                             * (half - max_exact)).astype(jnp.int32)
        large = jnp.minimum(large, half - 1)
        bucket = jnp.where(rel > 0, half, 0) + jnp.where(n < max_exact, n, large)
        allowed = ((key + LANES) // CHUNK - LANES // CHUNK) <= (query // CHUNK)
        for hh in heads:
            h = hp * ATT_HEADS + hh
            bias = jnp.zeros((w_keys, w_queries), F32)
            for i in range(N_BUCKETS):
                bias = jnp.where(bucket == i, relb_ref[i, h], bias)
            far = relb_ref[half - 1, h]
            window = jnp.where(allowed, (bias - far) * LOG2E, NEG_INF)
            for i in range(n_lane_tiles):
                cols = slice(i * LANES, (i + 1) * LANES)
                lo = (n_lane_tiles + i - 1) * LANES
                hi = min(lo + w_keys, 2 * t)
                bias_ref[hh, :lo, cols] = jnp.zeros((lo, LANES), F32)
                bias_ref[hh, lo:hi, cols] = window[:hi - lo, :]
                if hi < 2 * t:
                    bias_ref[hh, hi:, cols] = jnp.full((2 * t - hi, LANES), NEG_INF, F32)

    @pl.when(qi == 0)
    def _():
        for hh in heads:
            for j in range(v_ref.shape[0] // t):
                vj = v_ref[j * t:(j + 1) * t, lanes_of(hh)].astype(F32)
                vt_ref[hh, j, :HEAD_DV, :] = vj.T.astype(BF16)
                vt_ref[hh, j, HEAD_DV:, :] = jnp.ones((VT_ROWS - HEAD_DV, t), BF16)

    lane = lax.broadcasted_iota(jnp.int32, (t, HEAD_DV), 1)
    for hh in heads:
        q = q_ref[:, lanes_of(hh)]
        zero = jnp.zeros_like(q)
        qm_ref[hh, 0] = jnp.where(lane < HEAD_DK, q, zero)
        qm_ref[hh, 1] = jnp.where(lane >= HEAD_DK, q, zero)

    def qk_scores(hh, s_ref, mx_ref, tile, diagonal=False):
        kj = k_ref[pl.ds(pl.multiple_of(tile * t, t), t), lanes_of(hh)]
        if diagonal:
            bias = bias_ref[hh, t:, :]
        else:
            corner = jnp.where(tile == qi - 1, bias_ref[hh, t - LANES:t, :LANES], 0.0)
        for mi in range(2):
            s = lax.dot_general(kj, qm_ref[hh, mi], (((1,), (1,)), ((), ())),
                                preferred_element_type=F32)
            if diagonal:
                s = s + bias
            else:
                low = jnp.concatenate([s[t - LANES:, :LANES] + corner, s[t - LANES:, LANES:]],
                                      axis=1)
                s = jnp.concatenate([s[:t - LANES], low], axis=0)
            s_ref[hh, mi] = s
            mx_ref[hh, mi] = jnp.broadcast_to(jnp.max(s, axis=0, keepdims=True), (SUBLANES, t))

    def first_tile(hh, s_ref, mx_ref, tile):
        vt = vt_ref[hh, tile]
        for mi in range(2):
            m_new = mx_ref[hh, mi]
            p = jnp.exp2(s_ref[hh, mi] - m_new[:1])
            m_ref[hh, mi] = m_new
            acc_ref[hh, mi] = jnp.dot(vt, p.astype(BF16), preferred_element_type=F32)

    def next_tile(hh, s_ref, mx_ref, tile):
        vt = vt_ref[hh, tile]
        for mi in range(2):
            m_prev = m_ref[hh, mi]
            m_new = jnp.maximum(m_prev, mx_ref[hh, mi])
            alpha = jnp.exp2(m_prev - m_new)
            p = jnp.exp2(s_ref[hh, mi] - m_new[:1])
            m_ref[hh, mi] = m_new
            acc_ref[hh, mi] = (alpha[:1] * acc_ref[hh, mi]
                               + jnp.dot(vt, p.astype(BF16), preferred_element_type=F32))

    lam_init = scal_ref[0]
    lam = (jnp.exp(jnp.sum(lq1_ref[...] * lk1_ref[...], axis=1, keepdims=True))
           - jnp.exp(jnp.sum(lq2_ref[...] * lk2_ref[...], axis=1, keepdims=True))
           + lam_init)

    n_prev = qi
    last = jnp.maximum(n_prev - 1, 0)
    for hh in heads:
        qk_scores(hh, sa_ref, mxa_ref, qi, diagonal=True)
        qk_scores(hh, sb_ref, mxb_ref, 0)
        first_tile(hh, sa_ref, mxa_ref, qi)

        def two_tiles(first, hh=hh):
            qk_scores(hh, sa_ref, mxa_ref, jnp.minimum(first + 1, last))
            next_tile(hh, sb_ref, mxb_ref, first)
            qk_scores(hh, sb_ref, mxb_ref, jnp.minimum(first + 2, last))
            next_tile(hh, sa_ref, mxa_ref, first + 1)

        def run(start, count, pairs):
            def body(i, carry):
                for j in range(pairs):
                    two_tiles(start + 2 * pairs * i + 2 * j)
                return carry

            lax.fori_loop(0, count, body, 0)

        n8, n4 = n_prev // 8, (n_prev % 8) // 4
        run(0, n8, 4)
        run(8 * n8, n4, 2)
        run(8 * n8 + 4 * n4, (n_prev % 4) // 2, 1)

        @pl.when(n_prev % 2 == 1)
        def _(hh=hh):
            next_tile(hh, sb_ref, mxb_ref, last)

        o_t = (acc_ref[hh, 0, :HEAD_DV, :] / acc_ref[hh, 0, HEAD_DV:HEAD_DV + 1, :]
               - lam * (acc_ref[hh, 1, :HEAD_DV, :] / acc_ref[hh, 1, HEAD_DV:HEAD_DV + 1, :]))
        o = o_t.T
        o = o * lax.rsqrt(jnp.mean(o * o, axis=-1, keepdims=True) + EPS) * sg_ref[...]
        o = o * (1.0 - lam_init)
        z = z_ref[:, lanes_of(hh)].astype(F32)
        o_ref[:, lanes_of(hh)] = (o * (z * jax.nn.sigmoid(z))).astype(o_ref.dtype)


def _attention(proj, scal, rel_bias, lq1, lk1, lq2, lk2, subln_g):
    bsz, seq, _ = proj.shape
    t = ATT_T
    nh = ATT_HEADS
    width = nh * HEAD_DV
    groups = N_HEADS // nh
    smem = pl.BlockSpec(memory_space=pltpu.SMEM)
    vec = lambda n: pl.BlockSpec((1, n), lambda g, b, i: (0, 0))
    return pl.pallas_call(
        _attn_kernel,
        grid=(groups, bsz, seq // t),
        in_specs=[smem, smem, vec(HEAD_DK), vec(HEAD_DK), vec(HEAD_DK), vec(HEAD_DK),
                  vec(HEAD_DV),
                  pl.BlockSpec((None, t, width), lambda g, b, i: (b, i, g)),
                  pl.BlockSpec((None, seq, width), lambda g, b, i: (b, 0, groups + g)),
                  pl.BlockSpec((None, seq, width), lambda g, b, i: (b, 0, 2 * groups + g)),
                  pl.BlockSpec((None, t, width), lambda g, b, i: (b, i, 3 * groups + g))],
        out_specs=pl.BlockSpec((None, t, width), lambda g, b, i: (b, i, g)),
        out_shape=jax.ShapeDtypeStruct((bsz, seq, D_ATT), BF16),
        scratch_shapes=[pltpu.VMEM((nh, 2 * t, t), F32),
                        pltpu.VMEM((nh, 2, SUBLANES, t), F32),
                        pltpu.VMEM((nh, 2, VT_ROWS, t), F32),
                        pltpu.VMEM((nh, seq // t, VT_ROWS, t), BF16),
                        pltpu.VMEM((nh, 2, t, HEAD_DV), BF16),
                        pltpu.VMEM((nh, 2, t, t), F32),
                        pltpu.VMEM((nh, 2, t, t), F32),
                        pltpu.VMEM((nh, 2, SUBLANES, t), F32),
                        pltpu.VMEM((nh, 2, SUBLANES, t), F32)],
        compiler_params=pltpu.CompilerParams(
            dimension_semantics=("arbitrary", "arbitrary", "arbitrary"),
            vmem_limit_bytes=VMEM_LIMIT),
        name="diff_attention",
    )(scal, rel_bias, lq1, lk1, lq2, lk2, subln_g, proj, proj, proj, proj)


def _gelu_tanh(x):
    return 0.5 * x * (1.0 + jnp.tanh(math.sqrt(2.0 / math.pi) * (x + 0.044715 * (x * x * x))))


def _ssm_kernel(u_ref, z_ref, bw_ref, cw_ref, are_ref, aim_ref, d_ref, wglu_ref,
                o_ref, hre_ref, him_ref, st_ref, g_ref):
    nb = u_ref.shape[0]
    tb = SSM_TB
    pitch = SSM_PITCH
    pairs_per_chunk = MXU_DIM // (2 * SSM_GROUP)
    n_vregs = N_PAIRS // SUBLANES

    def pair_rows(b, gp):
        return slice((b * N_PAIRS + gp) * pitch, (b * N_PAIRS + gp) * pitch + tb)

    @pl.when(pl.program_id(0) == 0)
    def _():
        st_ref[...] = jnp.zeros_like(st_ref)

    for gp in range(N_PAIRS):
        kc = gp // pairs_per_chunk
        uc = jnp.concatenate([u_ref[b, :, kc * MXU_DIM:(kc + 1) * MXU_DIM] for b in range(nb)],
                             axis=0)
        bu = jnp.dot(uc, bw_ref[gp], preferred_element_type=F32)
        for b in range(nb):
            hre_ref[pair_rows(b, gp), :] = bu[b * tb:(b + 1) * tb, :LANES]
            him_ref[pair_rows(b, gp), :] = bu[b * tb:(b + 1) * tb, LANES:]

    a_re = [are_ref[k] for k in range(n_vregs)]
    a_im = [aim_ref[k] for k in range(n_vregs)]

    def step(ti, carry):
        out = []
        for b in range(nb):
            for k in range(n_vregs):
                h_re, h_im = carry[2 * (b * n_vregs + k)], carry[2 * (b * n_vregs + k) + 1]
                rows = pl.ds((b * N_PAIRS + k * SUBLANES) * pitch + ti, SUBLANES, stride=pitch)
                n_re = a_re[k] * h_re - a_im[k] * h_im + hre_ref[rows, :]
                n_im = a_re[k] * h_im + a_im[k] * h_re + him_ref[rows, :]
                hre_ref[rows, :] = n_re
                him_ref[rows, :] = n_im
                out += [n_re, n_im]
        return tuple(out)

    init = tuple(st_ref[i] for i in range(2 * nb * n_vregs))
    final = lax.fori_loop(0, tb, step, init, unroll=8)
    for i, s in enumerate(final):
        st_ref[i] = s

    for kc in range(D_SSM // MXU_DIM):
        gps = range(kc * pairs_per_chunk, (kc + 1) * pairs_per_chunk)
        hcat = jnp.concatenate(
            [jnp.concatenate([ref[pair_rows(b, gp), :]
                              for gp in gps for ref in (hre_ref, him_ref)], axis=1)
             for b in range(nb)], axis=0).astype(BF16)
        cwk = cw_ref[kc * pairs_per_chunk:(kc + 1) * pairs_per_chunk]
        acc = jnp.dot(hcat, cwk.reshape(pairs_per_chunk * MXU_DIM, MXU_DIM),
                      preferred_element_type=F32)
        cols = slice(kc * MXU_DIM, (kc + 1) * MXU_DIM)
        for b in range(nb):
            y = acc[b * tb:(b + 1) * tb] + d_ref[:, cols] * u_ref[b, :, cols].astype(F32)
            g_ref[b * tb:(b + 1) * tb, cols] = _gelu_tanh(y).astype(BF16)

    for kc in range(D_SSM // MXU_DIM):
        cols = slice(kc * MXU_DIM, (kc + 1) * MXU_DIM)
        gate_cols = slice(D_SSM + kc * MXU_DIM, D_SSM + (kc + 1) * MXU_DIM)
        val = jnp.dot(g_ref[...], wglu_ref[:, cols], preferred_element_type=F32)
        gate = jnp.dot(g_ref[...], wglu_ref[:, gate_cols], preferred_element_type=F32)
        for b in range(nb):
            rows = slice(b * tb, (b + 1) * tb)
            z = z_ref[b, :, cols].astype(F32)
            o = val[rows] * jax.nn.sigmoid(gate[rows]) * (z * jax.nn.sigmoid(z))
            o_ref[b, :, cols] = o.astype(o_ref.dtype)


def _ssm(proj, bw, cw, a_re_v, a_im_v, d_skip, w_glu):
    bsz, seq, _ = proj.shape
    tb = SSM_TB
    const = lambda shape: pl.BlockSpec(shape, lambda i: (0,) * len(shape),
                                       pipeline_mode=pl.Buffered(1))
    u_col = 4 * D_ATT // D_SSM
    n_state_vregs = 2 * bsz * (N_PAIRS // SUBLANES)
    return pl.pallas_call(
        _ssm_kernel,
        grid=(seq // tb,),
        in_specs=[pl.BlockSpec((bsz, tb, D_SSM), lambda i: (0, i, u_col)),
                  pl.BlockSpec((bsz, tb, D_SSM), lambda i: (0, i, u_col + 1)),
                  const((N_PAIRS, MXU_DIM, MXU_DIM)),
                  const((N_PAIRS, MXU_DIM, MXU_DIM)),
                  const((N_PAIRS // SUBLANES, SUBLANES, LANES)),
                  const((N_PAIRS // SUBLANES, SUBLANES, LANES)),
                  const((1, D_SSM)),
                  const((D_SSM, 2 * D_SSM))],
        out_specs=pl.BlockSpec((bsz, tb, D_SSM), lambda i: (0, i, 0)),
        out_shape=jax.ShapeDtypeStruct((bsz, seq, D_SSM), BF16),
        scratch_shapes=[pltpu.VMEM((bsz * N_PAIRS * SSM_PITCH, LANES), F32),
                        pltpu.VMEM((bsz * N_PAIRS * SSM_PITCH, LANES), F32),
                        pltpu.VMEM((n_state_vregs, SUBLANES, LANES), F32),
                        pltpu.VMEM((bsz * tb, D_SSM), BF16)],
        compiler_params=pltpu.CompilerParams(
            dimension_semantics=("arbitrary",),
            vmem_limit_bytes=VMEM_LIMIT),
        name="s5_glu",
    )(proj, proj, bw, cw, a_re_v, a_im_v, d_skip, w_glu)


def _ssm_weights(ab_re, ab_im, bb_re, bb_im, c_re, c_im):
    ppc = MXU_DIM // (2 * SSM_GROUP)
    sel = (jnp.arange(N_PAIRS)[:, None] % ppc == jnp.arange(ppc)[None, :]).astype(F32)
    group = jnp.arange(2)[None, :, None, None]

    def place(x_re, x_im):
        pieces = []
        for x in (x_re, x_im):
            xg = x.reshape(N_PAIRS, 2, SSM_GROUP, STATE)
            for gl2 in range(2):
                piece = jnp.where(group == gl2, xg, 0.0)[:, None] * sel[:, :, None, None, None]
                pieces.append(piece.reshape(N_PAIRS, ppc, 2 * SSM_GROUP, STATE))
        w = jnp.concatenate(pieces, axis=-1)
        return w.reshape(N_PAIRS, MXU_DIM, MXU_DIM)

    bw = place(bb_re, bb_im).astype(BF16)
    cw = jnp.swapaxes(place(c_re, -c_im), 1, 2).astype(BF16)
    vshape = (N_PAIRS // SUBLANES, SUBLANES, LANES)
    return bw, cw, ab_re.reshape(vshape), ab_im.reshape(vshape)


def _out_proj_kernel(oa_ref, os_ref, w_ref, g_ref, x_ref, o_ref):
    mix = (jnp.dot(oa_ref[...], w_ref[:D_ATT, :], preferred_element_type=F32)
           + jnp.dot(os_ref[...], w_ref[D_ATT:, :], preferred_element_type=F32))
    y = mix * lax.rsqrt(jnp.mean(mix * mix, axis=-1, keepdims=True) + EPS)
    o_ref[...] = x_ref[...] + y * g_ref[...]


def _out_proj(o_att, o_ssm, w, g, x2d):
    m = x2d.shape[0]
    return pl.pallas_call(
        _out_proj_kernel,
        grid=(m // OUT_TM,),
        in_specs=[pl.BlockSpec((OUT_TM, D_ATT), lambda i: (i, 0)),
                  pl.BlockSpec((OUT_TM, D_SSM), lambda i: (i, 0)),
                  pl.BlockSpec((D_MODEL, D_MODEL), lambda i: (0, 0),
                               pipeline_mode=pl.Buffered(1)),
                  pl.BlockSpec((1, D_MODEL), lambda i: (0, 0)),
                  pl.BlockSpec((OUT_TM, D_MODEL), lambda i: (i, 0))],
        out_specs=pl.BlockSpec((OUT_TM, D_MODEL), lambda i: (i, 0)),
        out_shape=jax.ShapeDtypeStruct((m, D_MODEL), F32),
        compiler_params=pltpu.CompilerParams(
            dimension_semantics=("arbitrary",),
            vmem_limit_bytes=VMEM_LIMIT),
        name="out_proj",
    )(o_att, o_ssm, w, g, x2d)


def kernel(x, rel_bias, pre_norm_g, post_norm_g, w_in, lambda_q1, lambda_k1, lambda_q2,
           lambda_k2, subln_g, ssm_a_re, ssm_a_im, ssm_log_dt, ssm_b_re, ssm_b_im,
           ssm_c_re, ssm_c_im, ssm_d, w_glu, w_out):
    bsz, seq, _ = x.shape
    m = bsz * seq
    ab_re, ab_im, bb_re, bb_im = _discretize(ssm_a_re, ssm_a_im, ssm_log_dt, ssm_b_re, ssm_b_im)
    col_scale = jnp.concatenate([jnp.full((D_ATT,), HEAD_DK ** -0.5 * LOG2E, F32),
                                 jnp.ones((D_IN - D_ATT,), F32)])
    x2d = x.reshape(m, D_MODEL)
    for l in range(DEPTH):
        lam_init = _lambda_init(l)
        w_in_l = (w_in[l] * col_scale).astype(BF16)
        proj = _in_proj(x2d, pre_norm_g[l].reshape(1, D_MODEL), w_in_l)
        proj = proj.reshape(bsz, seq, D_IN)
        scal = jnp.array([lam_init], F32)
        o_att = _attention(proj, scal, rel_bias,
                           lambda_q1[l].reshape(1, HEAD_DK), lambda_k1[l].reshape(1, HEAD_DK),
                           lambda_q2[l].reshape(1, HEAD_DK), lambda_k2[l].reshape(1, HEAD_DK),
                           subln_g[l].reshape(1, HEAD_DV))
        bw, cw, a_re_v, a_im_v = _ssm_weights(
            ab_re[l, :, 0, :], ab_im[l, :, 0, :], bb_re[l], bb_im[l], ssm_c_re[l], ssm_c_im[l])
        o_ssm = _ssm(proj, bw, cw, a_re_v, a_im_v, ssm_d[l].reshape(1, D_SSM),
                     w_glu[l].astype(BF16))
        x2d = _out_proj(o_att.reshape(m, D_ATT), o_ssm.reshape(m, D_SSM),
                        w_out[l].astype(BF16), post_norm_g[l].reshape(1, D_MODEL), x2d)
    return x2d.reshape(bsz, seq, D_MODEL)
```

```python
import math

import jax
import jax.numpy as jnp
from jax import lax
from jax.experimental import pallas as pl
from jax.experimental.pallas import tpu as pltpu

F32 = jnp.float32
BF16 = jnp.bfloat16

D_MODEL = 2048
DEPTH = 4
CHUNK = 64
D_ATT = 1024
D_SSM = 1024
N_HEADS = 8
HEAD_DV = 128
HEAD_DK = 64
SSM_GROUP = 16
N_GROUPS = 64
STATE = 64
N_BUCKETS = 32
MAX_DISTANCE = 128
EPS = 1e-6
NEG_INF = -1e30
LOG2E = math.log2(math.e)
D_IN = 4 * D_ATT + 2 * D_SSM

LANES = 128
SUBLANES = 8
BF16_ROWS = 16
MXU_DIM = 256
VMEM_LIMIT = 56 * 1024 * 1024

PROJ_TM = 1024
PROJ_TN = 1536
ATT_T = 512
ATT_HEADS = 2
VT_ROWS = HEAD_DV + BF16_ROWS
SSM_TB = 256
SSM_PITCH = SSM_TB + SUBLANES // 2
N_PAIRS = N_GROUPS // 2
SCAN_UNROLL = 8
OUT_TM = 512


def _lambda_init(layer_idx):
    return 0.8 - 0.6 * math.exp(-0.3 * layer_idx)


def _discretize_kernel(a_re_ref, a_im_ref, log_dt_ref, b_re_ref, b_im_ref,
                       ab_re_ref, ab_im_ref, bb_re_ref, bb_im_ref):
    a_re = a_re_ref[...]
    a_im = a_im_ref[...]
    dt = jnp.exp(log_dt_ref[...])
    mag = jnp.exp(dt * a_re)
    ab_re = mag * jnp.cos(dt * a_im)
    ab_im = mag * jnp.sin(dt * a_im)
    den = a_re * a_re + a_im * a_im
    nr = ab_re - 1.0
    cf_re = (nr * a_re + ab_im * a_im) / den
    cf_im = (ab_im * a_re - nr * a_im) / den
    b_re = b_re_ref[...]
    b_im = b_im_ref[...]
    ab_re_ref[...] = ab_re
    ab_im_ref[...] = ab_im
    bb_re_ref[...] = cf_re * b_re - cf_im * b_im
    bb_im_ref[...] = cf_re * b_im + cf_im * b_re


def _discretize(a_re, a_im, log_dt, b_re, b_im):
    nl = a_re.shape[0]
    a4 = (nl, N_GROUPS, 1, STATE)
    b4 = (nl, N_GROUPS, SSM_GROUP, STATE)
    return pl.pallas_call(
        _discretize_kernel,
        out_shape=(jax.ShapeDtypeStruct(a4, F32), jax.ShapeDtypeStruct(a4, F32),
                   jax.ShapeDtypeStruct(b4, F32), jax.ShapeDtypeStruct(b4, F32)),
        name="s5_discretize",
    )(a_re.reshape(a4), a_im.reshape(a4), log_dt.reshape(nl, N_GROUPS, 1, 1),
      jnp.swapaxes(b_re, 2, 3), jnp.swapaxes(b_im, 2, 3))


def _in_proj_kernel(x_ref, g_ref, w_ref, o_ref, h_ref):
    @pl.when(pl.program_id(1) == 0)
    def _():
        x = x_ref[...]
        y = x * lax.rsqrt(jnp.mean(x * x, axis=-1, keepdims=True) + EPS)
        h_ref[...] = (y * g_ref[...]).astype(BF16)

    o_ref[...] = jnp.dot(h_ref[...], w_ref[...],
                         preferred_element_type=F32).astype(o_ref.dtype)


def _in_proj(x2d, g, w):
    m = x2d.shape[0]
    return pl.pallas_call(
        _in_proj_kernel,
        grid=(m // PROJ_TM, D_IN // PROJ_TN),
        in_specs=[pl.BlockSpec((PROJ_TM, D_MODEL), lambda i, j: (i, 0)),
                  pl.BlockSpec((1, D_MODEL), lambda i, j: (0, 0)),
                  pl.BlockSpec((D_MODEL, PROJ_TN), lambda i, j: (0, j))],
        out_specs=pl.BlockSpec((PROJ_TM, PROJ_TN), lambda i, j: (i, j)),
        out_shape=jax.ShapeDtypeStruct((m, D_IN), BF16),
        scratch_shapes=[pltpu.VMEM((PROJ_TM, D_MODEL), BF16)],
        compiler_params=pltpu.CompilerParams(
            dimension_semantics=("arbitrary", "arbitrary"),
            vmem_limit_bytes=VMEM_LIMIT),
        name="in_proj",
    )(x2d, g, w)


def _attn_kernel(scal_ref, relb_ref, lq1_ref, lk1_ref, lq2_ref, lk2_ref, sg_ref,
                 q_ref, k_ref, v_ref, z_ref, o_ref,
                 bias_ref, m_ref, acc_ref, vt_ref, qm_ref, sa_ref, sb_ref, mxa_ref, mxb_ref):
    t = ATT_T
    n_lane_tiles = t // LANES
    heads = range(ATT_HEADS)
    hp = pl.program_id(0)
    b = pl.program_id(1)
    qi = pl.program_id(2)
    lanes_of = lambda hh: slice(hh * HEAD_DV, (hh + 1) * HEAD_DV)

    @pl.when((b == 0) & (qi == 0))
    def _():
        w_keys, w_queries = 3 * LANES, LANES
        key = lax.broadcasted_iota(jnp.int32, (w_keys, w_queries), 0) - LANES
        query = lax.broadcasted_iota(jnp.int32, (w_keys, w_queries), 1)
        rel = key - query
        half = N_BUCKETS // 2
        max_exact = half // 2
        n = jnp.abs(rel)
        nf = jnp.maximum(n, 1).astype(F32)
        large = max_exact + (jnp.log(nf / max_exact) / math.log(MAX_DISTANCE / max_exact)
                             * (half - max_exact)).astype(jnp.int32)
        large = jnp.minimum(large, half - 1)
        bucket = jnp.where(rel > 0, half, 0) + jnp.where(n < max_exact, n, large)
        allowed = ((key + LANES) // CHUNK - LANES // CHUNK) <= (query // CHUNK)
        for hh in heads:
            h = hp * ATT_HEADS + hh
            bias = jnp.zeros((w_keys, w_queries), F32)
            for i in range(N_BUCKETS):
                bias = jnp.where(bucket == i, relb_ref[i, h], bias)
            far = relb_ref[half - 1, h]
            window = jnp.where(allowed, (bias - far) * LOG2E, NEG_INF)
            for i in range(n_lane_tiles):
                cols = slice(i * LANES, (i + 1) * LANES)
                lo = (n_lane_tiles + i - 1) * LANES
                hi = min(lo + w_keys, 2 * t)
                bias_ref[hh, :lo, cols] = jnp.zeros((lo, LANES), F32)
                bias_ref[hh, lo:hi, cols] = window[:hi - lo, :]
                if hi < 2 * t:
                    bias_ref[hh, hi:, cols] = jnp.full((2 * t - hi, LANES), NEG_INF, F32)

    @pl.when(qi == 0)
    def _():
        for hh in heads:
            for j in range(v_ref.shape[0] // t):
                vj = v_ref[j * t:(j + 1) * t, lanes_of(hh)].astype(F32)
                vt_ref[hh, j, :HEAD_DV, :] = vj.T.astype(BF16)
                vt_ref[hh, j, HEAD_DV:, :] = jnp.ones((VT_ROWS - HEAD_DV, t), BF16)

    lane = lax.broadcasted_iota(jnp.int32, (t, HEAD_DV), 1)
    for hh in heads:
        q = q_ref[:, lanes_of(hh)]
        zero = jnp.zeros_like(q)
        qm_ref[hh, 0] = jnp.where(lane < HEAD_DK, q, zero)
        qm_ref[hh, 1] = jnp.where(lane >= HEAD_DK, q, zero)

    def qk_scores(hh, s_ref, mx_ref, tile, diagonal=False):
        kj = k_ref[pl.ds(pl.multiple_of(tile * t, t), t), lanes_of(hh)]
        if diagonal:
            bias = bias_ref[hh, t:, :]
        else:
            corner = jnp.where(tile == qi - 1, bias_ref[hh, t - LANES:t, :LANES], 0.0)
        for mi in range(2):
            s = lax.dot_general(kj, qm_ref[hh, mi], (((1,), (1,)), ((), ())),
                                preferred_element_type=F32)
            if diagonal:
                s = s + bias
            else:
                low = jnp.concatenate([s[t - LANES:, :LANES] + corner, s[t - LANES:, LANES:]],
                                      axis=1)
                s = jnp.concatenate([s[:t - LANES], low], axis=0)
            s_ref[hh, mi] = s
            mx_ref[hh, mi] = jnp.broadcast_to(jnp.max(s, axis=0, keepdims=True), (SUBLANES, t))

    def first_tile(hh, s_ref, mx_ref, tile):
        vt = vt_ref[hh, tile]
        for mi in range(2):
            m_new = mx_ref[hh, mi]
            p = jnp.exp2(s_ref[hh, mi] - m_new[:1])
            m_ref[hh, mi] = m_new
            acc_ref[hh, mi] = jnp.dot(vt, p.astype(BF16), preferred_element_type=F32)

    def next_tile(hh, s_ref, mx_ref, tile):
        vt = vt_ref[hh, tile]
        for mi in range(2):
            m_prev = m_ref[hh, mi]
            m_new = jnp.maximum(m_prev, mx_ref[hh, mi])
            alpha = jnp.exp2(m_prev - m_new)
            p = jnp.exp2(s_ref[hh, mi] - m_new[:1])
            m_ref[hh, mi] = m_new
            acc_ref[hh, mi] = (alpha[:1] * acc_ref[hh, mi]
                               + jnp.dot(vt, p.astype(BF16), preferred_element_type=F32))

    lam_init = scal_ref[0]
    lam = (jnp.exp(jnp.sum(lq1_ref[...] * lk1_ref[...], axis=1, keepdims=True))
           - jnp.exp(jnp.sum(lq2_ref[...] * lk2_ref[...], axis=1, keepdims=True))
           + lam_init)

    n_prev = qi
    last = jnp.maximum(n_prev - 1, 0)
    for hh in heads:
        qk_scores(hh, sa_ref, mxa_ref, qi, diagonal=True)
        qk_scores(hh, sb_ref, mxb_ref, 0)
        first_tile(hh, sa_ref, mxa_ref, qi)

        def two_tiles(first, hh=hh):
            qk_scores(hh, sa_ref, mxa_ref, jnp.minimum(first + 1, last))
            next_tile(hh, sb_ref, mxb_ref, first)
            qk_scores(hh, sb_ref, mxb_ref, jnp.minimum(first + 2, last))
            next_tile(hh, sa_ref, mxa_ref, first + 1)

        def run(start, count, pairs):
            def body(i, carry):
                for j in range(pairs):
                    two_tiles(start + 2 * pairs * i + 2 * j)
                return carry

            lax.fori_loop(0, count, body, 0)

        n8, n4 = n_prev // 8, (n_prev % 8) // 4
        run(0, n8, 4)
        run(8 * n8, n4, 2)
        run(8 * n8 + 4 * n4, (n_prev % 4) // 2, 1)

        @pl.when(n_prev % 2 == 1)
        def _(hh=hh):
            next_tile(hh, sb_ref, mxb_ref, last)

        o_t = (acc_ref[hh, 0, :HEAD_DV, :] / acc_ref[hh, 0, HEAD_DV:HEAD_DV + 1, :]
               - lam * (acc_ref[hh, 1, :HEAD_DV, :] / acc_ref[hh, 1, HEAD_DV:HEAD_DV + 1, :]))
        o = o_t.T
        o = o * lax.rsqrt(jnp.mean(o * o, axis=-1, keepdims=True) + EPS) * sg_ref[...]
        o = o * (1.0 - lam_init)
        z = z_ref[:, lanes_of(hh)].astype(F32)
        o_ref[:, lanes_of(hh)] = (o * (z * jax.nn.sigmoid(z))).astype(o_ref.dtype)


def _attention(proj, scal, rel_bias, lq1, lk1, lq2, lk2, subln_g):
    bsz, seq, _ = proj.shape
    t = ATT_T
    nh = ATT_HEADS
    width = nh * HEAD_DV
    groups = N_HEADS // nh
    smem = pl.BlockSpec(memory_space=pltpu.SMEM)
    vec = lambda n: pl.BlockSpec((1, n), lambda g, b, i: (0, 0))
    return pl.pallas_call(
        _attn_kernel,
        grid=(groups, bsz, seq // t),
        in_specs=[smem, smem, vec(HEAD_DK), vec(HEAD_DK), vec(HEAD_DK), vec(HEAD_DK),
                  vec(HEAD_DV),
                  pl.BlockSpec((None, t, width), lambda g, b, i: (b, i, g)),
                  pl.BlockSpec((None, seq, width), lambda g, b, i: (b, 0, groups + g)),
                  pl.BlockSpec((None, seq, width), lambda g, b, i: (b, 0, 2 * groups + g)),
                  pl.BlockSpec((None, t, width), lambda g, b, i: (b, i, 3 * groups + g))],
        out_specs=pl.BlockSpec((None, t, width), lambda g, b, i: (b, i, g)),
        out_shape=jax.ShapeDtypeStruct((bsz, seq, D_ATT), BF16),
        scratch_shapes=[pltpu.VMEM((nh, 2 * t, t), F32),
                        pltpu.VMEM((nh, 2, SUBLANES, t), F32),
                        pltpu.VMEM((nh, 2, VT_ROWS, t), F32),
                        pltpu.VMEM((nh, seq // t, VT_ROWS, t), BF16),
                        pltpu.VMEM((nh, 2, t, HEAD_DV), BF16),
                        pltpu.VMEM((nh, 2, t, t), F32),
                        pltpu.VMEM((nh, 2, t, t), F32),
                        pltpu.VMEM((nh, 2, SUBLANES, t), F32),
                        pltpu.VMEM((nh, 2, SUBLANES, t), F32)],
        compiler_params=pltpu.CompilerParams(
            dimension_semantics=("arbitrary", "arbitrary", "arbitrary"),
            vmem_limit_bytes=VMEM_LIMIT),
        name="diff_attention",
    )(scal, rel_bias, lq1, lk1, lq2, lk2, subln_g, proj, proj, proj, proj)


def _gelu_tanh(x):
    return 0.5 * x * (1.0 + jnp.tanh(math.sqrt(2.0 / math.pi) * (x + 0.044715 * (x * x * x))))


def _ssm_kernel(u_ref, z_ref, bw_ref, cw_ref, are_ref, aim_ref, d_ref, wglu_ref,
                o_ref, hre_ref, him_ref, st_ref, g_ref):
    nb = u_ref.shape[0]
    tb = SSM_TB
    pitch = SSM_PITCH
    pairs_per_chunk = MXU_DIM // (2 * SSM_GROUP)
    n_vregs = N_PAIRS // SUBLANES

    def pair_rows(b, gp):
        return slice((b * N_PAIRS + gp) * pitch, (b * N_PAIRS + gp) * pitch + tb)

    @pl.when(pl.program_id(0) == 0)
    def _():
        st_ref[...] = jnp.zeros_like(st_ref)

    for gp in range(N_PAIRS):
        kc = gp // pairs_per_chunk
        uc = jnp.concatenate([u_ref[b, :, kc * MXU_DIM:(kc + 1) * MXU_DIM] for b in range(nb)],
                             axis=0)
        bu = jnp.dot(uc, bw_ref[gp], preferred_element_type=F32)
        for b in range(nb):
            hre_ref[pair_rows(b, gp), :] = bu[b * tb:(b + 1) * tb, :LANES]
            him_ref[pair_rows(b, gp), :] = bu[b * tb:(b + 1) * tb, LANES:]

    a_re = [are_ref[k] for k in range(n_vregs)]
    a_im = [aim_ref[k] for k in range(n_vregs)]

    def step(ti, carry):
        out = []
        for b in range(nb):
            for k in range(n_vregs):
                h_re, h_im = carry[2 * (b * n_vregs + k)], carry[2 * (b * n_vregs + k) + 1]
                rows = pl.ds((b * N_PAIRS + k * SUBLANES) * pitch + ti, SUBLANES, stride=pitch)
                n_re = a_re[k] * h_re - a_im[k] * h_im + hre_ref[rows, :]
                n_im = a_re[k] * h_im + a_im[k] * h_re + him_ref[rows, :]
                hre_ref[rows, :] = n_re
                him_ref[rows, :] = n_im
                out += [n_re, n_im]
        return tuple(out)

    init = tuple(st_ref[i] for i in range(2 * nb * n_vregs))
    final = lax.fori_loop(0, tb, step, init, unroll=SCAN_UNROLL)
    for i, s in enumerate(final):
        st_ref[i] = s

    for kc in range(D_SSM // MXU_DIM):
        gps = range(kc * pairs_per_chunk, (kc + 1) * pairs_per_chunk)
        hcat = jnp.concatenate(
            [jnp.concatenate([ref[pair_rows(b, gp), :]
                              for gp in gps for ref in (hre_ref, him_ref)], axis=1)
             for b in range(nb)], axis=0).astype(BF16)
        cwk = cw_ref[kc * pairs_per_chunk:(kc + 1) * pairs_per_chunk]
        acc = jnp.dot(hcat, cwk.reshape(pairs_per_chunk * MXU_DIM, MXU_DIM),
                      preferred_element_type=F32)
        cols = slice(kc * MXU_DIM, (kc + 1) * MXU_DIM)
        for b in range(nb):
            y = acc[b * tb:(b + 1) * tb] + d_ref[:, cols] * u_ref[b, :, cols].astype(F32)
            g_ref[b * tb:(b + 1) * tb, cols] = _gelu_tanh(y).astype(BF16)

    for kc in range(D_SSM // MXU_DIM):
        cols = slice(kc * MXU_DIM, (kc + 1) * MXU_DIM)
        gate_cols = slice(D_SSM + kc * MXU_DIM, D_SSM + (kc + 1) * MXU_DIM)
        val = jnp.dot(g_ref[...], wglu_ref[:, cols], preferred_element_type=F32)
        gate = jnp.dot(g_ref[...], wglu_ref[:, gate_cols], preferred_element_type=F32)
        for b in range(nb):
            rows = slice(b * tb, (b + 1) * tb)
            z = z_ref[b, :, cols].astype(F32)
            o = val[rows] * jax.nn.sigmoid(gate[rows]) * (z * jax.nn.sigmoid(z))
            o_ref[b, :, cols] = o.astype(o_ref.dtype)


def _ssm(proj, bw, cw, a_re_v, a_im_v, d_skip, w_glu):
    bsz, seq, _ = proj.shape
    tb = SSM_TB
    const = lambda shape: pl.BlockSpec(shape, lambda i: (0,) * len(shape),
                                       pipeline_mode=pl.Buffered(1))
    u_col = 4 * D_ATT // D_SSM
    n_state_vregs = 2 * bsz * (N_PAIRS // SUBLANES)
    return pl.pallas_call(
        _ssm_kernel,
        grid=(seq // tb,),
        in_specs=[pl.BlockSpec((bsz, tb, D_SSM), lambda i: (0, i, u_col)),
                  pl.BlockSpec((bsz, tb, D_SSM), lambda i: (0, i, u_col + 1)),
                  const((N_PAIRS, MXU_DIM, MXU_DIM)),
                  const((N_PAIRS, MXU_DIM, MXU_DIM)),
                  const((N_PAIRS // SUBLANES, SUBLANES, LANES)),
                  const((N_PAIRS // SUBLANES, SUBLANES, LANES)),
                  const((1, D_SSM)),
                  const((D_SSM, 2 * D_SSM))],
        out_specs=pl.BlockSpec((bsz, tb, D_SSM), lambda i: (0, i, 0)),
        out_shape=jax.ShapeDtypeStruct((bsz, seq, D_SSM), BF16),
        scratch_shapes=[pltpu.VMEM((bsz * N_PAIRS * SSM_PITCH, LANES), F32),
                        pltpu.VMEM((bsz * N_PAIRS * SSM_PITCH, LANES), F32),
                        pltpu.VMEM((n_state_vregs, SUBLANES, LANES), F32),
                        pltpu.VMEM((bsz * tb, D_SSM), BF16)],
        compiler_params=pltpu.CompilerParams(
            dimension_semantics=("arbitrary",),
            vmem_limit_bytes=VMEM_LIMIT),
        name="s5_glu",
    )(proj, proj, bw, cw, a_re_v, a_im_v, d_skip, w_glu)


def _ssm_weights(ab_re, ab_im, bb_re, bb_im, c_re, c_im):
    ppc = MXU_DIM // (2 * SSM_GROUP)
    sel = (jnp.arange(N_PAIRS)[:, None] % ppc == jnp.arange(ppc)[None, :]).astype(F32)
    group = jnp.arange(2)[None, :, None, None]

    def place(x_re, x_im):
        pieces = []
        for x in (x_re, x_im):
            xg = x.reshape(N_PAIRS, 2, SSM_GROUP, STATE)
            for gl2 in range(2):
                piece = jnp.where(group == gl2, xg, 0.0)[:, None] * sel[:, :, None, None, None]
                pieces.append(piece.reshape(N_PAIRS, ppc, 2 * SSM_GROUP, STATE))
        w = jnp.concatenate(pieces, axis=-1)
        return w.reshape(N_PAIRS, MXU_DIM, MXU_DIM)

    bw = place(bb_re, bb_im).astype(BF16)
    cw = jnp.swapaxes(place(c_re, -c_im), 1, 2).astype(BF16)
    vshape = (N_PAIRS // SUBLANES, SUBLANES, LANES)
    return bw, cw, ab_re.reshape(vshape), ab_im.reshape(vshape)


def _out_proj_kernel(oa_ref, os_ref, w_ref, g_ref, x_ref, o_ref):
    mix = (jnp.dot(oa_ref[...], w_ref[:D_ATT, :], preferred_element_type=F32)
           + jnp.dot(os_ref[...], w_ref[D_ATT:, :], preferred_element_type=F32))
    y = mix * lax.rsqrt(jnp.mean(mix * mix, axis=-1, keepdims=True) + EPS)
    o_ref[...] = x_ref[...] + y * g_ref[...]


def _out_proj(o_att, o_ssm, w, g, x2d):
    m = x2d.shape[0]
    return pl.pallas_call(
        _out_proj_kernel,
        grid=(m // OUT_TM,),
        in_specs=[pl.BlockSpec((OUT_TM, D_ATT), lambda i: (i, 0)),
                  pl.BlockSpec((OUT_TM, D_SSM), lambda i: (i, 0)),
                  pl.BlockSpec((D_MODEL, D_MODEL), lambda i: (0, 0),
                               pipeline_mode=pl.Buffered(1)),
                  pl.BlockSpec((1, D_MODEL), lambda i: (0, 0)),
                  pl.BlockSpec((OUT_TM, D_MODEL), lambda i: (i, 0))],
        out_specs=pl.BlockSpec((OUT_TM, D_MODEL), lambda i: (i, 0)),
        out_shape=jax.ShapeDtypeStruct((m, D_MODEL), F32),
        compiler_params=pltpu.CompilerParams(
            dimension_semantics=("arbitrary",),
            vmem_limit_bytes=VMEM_LIMIT),
        name="out_proj",
    )(o_att, o_ssm, w, g, x2d)


def kernel(x, rel_bias, pre_norm_g, post_norm_g, w_in, lambda_q1, lambda_k1, lambda_q2,
           lambda_k2, subln_g, ssm_a_re, ssm_a_im, ssm_log_dt, ssm_b_re, ssm_b_im,
           ssm_c_re, ssm_c_im, ssm_d, w_glu, w_out):
    bsz, seq, _ = x.shape
    m = bsz * seq
    ab_re, ab_im, bb_re, bb_im = _discretize(ssm_a_re, ssm_a_im, ssm_log_dt, ssm_b_re, ssm_b_im)
    col_scale = jnp.concatenate([jnp.full((D_ATT,), HEAD_DK ** -0.5 * LOG2E, F32),
                                 jnp.ones((D_IN - D_ATT,), F32)])
    x2d = x.reshape(m, D_MODEL)
    for l in range(DEPTH):
        lam_init = _lambda_init(l)
        w_in_l = (w_in[l] * col_scale).astype(BF16)
        proj = _in_proj(x2d, pre_norm_g[l].reshape(1, D_MODEL), w_in_l)
        proj = proj.reshape(bsz, seq, D_IN)
        scal = jnp.array([lam_init], F32)
        o_att = _attention(proj, scal, rel_bias,
                           lambda_q1[l].reshape(1, HEAD_DK), lambda_k1[l].reshape(1, HEAD_DK),
                           lambda_q2[l].reshape(1, HEAD_DK), lambda_k2[l].reshape(1, HEAD_DK),
                           subln_g[l].reshape(1, HEAD_DV))
        bw, cw, a_re_v, a_im_v = _ssm_weights(
            ab_re[l, :, 0, :], ab_im[l, :, 0, :], bb_re[l], bb_im[l], ssm_c_re[l], ssm_c_im[l])
        o_ssm = _ssm(proj, bw, cw, a_re_v, a_im_v, ssm_d[l].reshape(1, D_SSM),
                     w_glu[l].astype(BF16))
        x2d = _out_proj(o_att.reshape(m, D_ATT), o_ssm.reshape(m, D_SSM),
                        w_out[l].astype(BF16), post_norm_g[l].reshape(1, D_MODEL), x2d)
    return x2d.reshape(bsz, seq, D_MODEL)
```

```python
import math

import jax
import jax.numpy as jnp
from jax import lax
from jax.experimental import pallas as pl
from jax.experimental.pallas import tpu as pltpu

F32 = jnp.float32
BF16 = jnp.bfloat16

D_MODEL = 2048
DEPTH = 4
CHUNK = 64
D_ATT = 1024
D_SSM = 1024
N_HEADS = 8
HEAD_DV = 128
HEAD_DK = 64
SSM_GROUP = 16
N_GROUPS = 64
STATE = 64
N_BUCKETS = 32
MAX_DISTANCE = 128
EPS = 1e-6
NEG_INF = -1e30
LOG2E = math.log2(math.e)
D_IN = 4 * D_ATT + 2 * D_SSM

LANES = 128
SUBLANES = 8
BF16_ROWS = 16
MXU_DIM = 256
VMEM_LIMIT = 56 * 1024 * 1024

PROJ_TM = 1024
PROJ_TN = 1536
ATT_T = 512
ATT_HEADS = 2
VT_ROWS = HEAD_DV + BF16_ROWS
SSM_TB = 256
SSM_PITCH = SSM_TB + SUBLANES // 2
N_PAIRS = N_GROUPS // 2
SCAN_UNROLL = 8
OUT_TM = 512


def _lambda_init(layer_idx):
    return 0.8 - 0.6 * math.exp(-0.3 * layer_idx)


def _discretize_kernel(a_re_ref, a_im_ref, log_dt_ref, b_re_ref, b_im_ref,
                       ab_re_ref, ab_im_ref, bb_re_ref, bb_im_ref):
    a_re = a_re_ref[...]
    a_im = a_im_ref[...]
    dt = jnp.exp(log_dt_ref[...])
    mag = jnp.exp(dt * a_re)
    ab_re = mag * jnp.cos(dt * a_im)
    ab_im = mag * jnp.sin(dt * a_im)
    den = a_re * a_re + a_im * a_im
    nr = ab_re - 1.0
    cf_re = (nr * a_re + ab_im * a_im) / den
    cf_im = (ab_im * a_re - nr * a_im) / den
    b_re = b_re_ref[...]
    b_im = b_im_ref[...]
    ab_re_ref[...] = ab_re
    ab_im_ref[...] = ab_im
    bb_re_ref[...] = cf_re * b_re - cf_im * b_im
    bb_im_ref[...] = cf_re * b_im + cf_im * b_re


def _discretize(a_re, a_im, log_dt, b_re, b_im):
    nl = a_re.shape[0]
    a4 = (nl, N_GROUPS, 1, STATE)
    b4 = (nl, N_GROUPS, SSM_GROUP, STATE)
    return pl.pallas_call(
        _discretize_kernel,
        out_shape=(jax.ShapeDtypeStruct(a4, F32), jax.ShapeDtypeStruct(a4, F32),
                   jax.ShapeDtypeStruct(b4, F32), jax.ShapeDtypeStruct(b4, F32)),
        name="s5_discretize",
    )(a_re.reshape(a4), a_im.reshape(a4), log_dt.reshape(nl, N_GROUPS, 1, 1),
      jnp.swapaxes(b_re, 2, 3), jnp.swapaxes(b_im, 2, 3))


def _in_proj_kernel(x_ref, g_ref, w_ref, o_ref, h_ref):
    @pl.when(pl.program_id(1) == 0)
    def _():
        x = x_ref[...]
        y = x * lax.rsqrt(jnp.mean(x * x, axis=-1, keepdims=True) + EPS)
        h_ref[...] = (y * g_ref[...]).astype(BF16)

    o_ref[...] = jnp.dot(h_ref[...], w_ref[...],
                         preferred_element_type=F32).astype(o_ref.dtype)


def _in_proj(x2d, g, w):
    m = x2d.shape[0]
    return pl.pallas_call(
        _in_proj_kernel,
        grid=(m // PROJ_TM, D_IN // PROJ_TN),
        in_specs=[pl.BlockSpec((PROJ_TM, D_MODEL), lambda i, j: (i, 0)),
                  pl.BlockSpec((1, D_MODEL), lambda i, j: (0, 0)),
                  pl.BlockSpec((D_MODEL, PROJ_TN), lambda i, j: (0, j))],
        out_specs=pl.BlockSpec((PROJ_TM, PROJ_TN), lambda i, j: (i, j)),
        out_shape=jax.ShapeDtypeStruct((m, D_IN), BF16),
        scratch_shapes=[pltpu.VMEM((PROJ_TM, D_MODEL), BF16)],
        compiler_params=pltpu.CompilerParams(
            dimension_semantics=("arbitrary", "arbitrary"),
            vmem_limit_bytes=VMEM_LIMIT),
        name="in_proj",
    )(x2d, g, w)


def _attn_kernel(scal_ref, relb_ref, lq1_ref, lk1_ref, lq2_ref, lk2_ref, sg_ref,
                 q_ref, k_ref, v_ref, z_ref, o_ref,
                 bias_ref, m_ref, acc_ref, vt_ref, qm_ref, sa_ref, sb_ref, mxa_ref, mxb_ref):
    t = ATT_T
    n_lane_tiles = t // LANES
    heads = range(ATT_HEADS)
    hp = pl.program_id(0)
    b = pl.program_id(1)
    qi = pl.program_id(2)
    lanes_of = lambda hh: slice(hh * HEAD_DV, (hh + 1) * HEAD_DV)

    @pl.when((b == 0) & (qi == 0))
    def _():
        w_keys, w_queries = 3 * LANES, LANES
        key = lax.broadcasted_iota(jnp.int32, (w_keys, w_queries), 0) - LANES
        query = lax.broadcasted_iota(jnp.int32, (w_keys, w_queries), 1)
        rel = key - query
        half = N_BUCKETS // 2
        max_exact = half // 2
        n = jnp.abs(rel)
        nf = jnp.maximum(n, 1).astype(F32)
        large = max_exact + (jnp.log(nf / max_exact) / math.log(MAX_DISTANCE / max_exact)
                             * (half - max_exact)).astype(jnp.int32)
        large = jnp.minimum(large, half - 1)
        bucket = jnp.where(rel > 0, half, 0) + jnp.where(n < max_exact, n, large)
        allowed = ((key + LANES) // CHUNK - LANES // CHUNK) <= (query // CHUNK)
        for hh in heads:
            h = hp * ATT_HEADS + hh
            bias = jnp.zeros((w_keys, w_queries), F32)
            for i in range(N_BUCKETS):
                bias = jnp.where(bucket == i, relb_ref[i, h], bias)
            far = relb_ref[half - 1, h]
            window = jnp.where(allowed, (bias - far) * LOG2E, NEG_INF)
            for i in range(n_lane_tiles):
                cols = slice(i * LANES, (i + 1) * LANES)
                lo = (n_lane_tiles + i - 1) * LANES
                hi = min(lo + w_keys, 2 * t)
                bias_ref[hh, :lo, cols] = jnp.zeros((lo, LANES), F32)
                bias_ref[hh, lo:hi, cols] = window[:hi - lo, :]
                if hi < 2 * t:
                    bias_ref[hh, hi:, cols] = jnp.full((2 * t - hi, LANES), NEG_INF, F32)

    @pl.when(qi == 0)
    def _():
        for hh in heads:
            for j in range(v_ref.shape[0] // t):
                vj = v_ref[j * t:(j + 1) * t, lanes_of(hh)].astype(F32)
                vt_ref[hh, j, :HEAD_DV, :] = vj.T.astype(BF16)
                vt_ref[hh, j, HEAD_DV:, :] = jnp.ones((VT_ROWS - HEAD_DV, t), BF16)

    lane = lax.broadcasted_iota(jnp.int32, (t, HEAD_DV), 1)
    for hh in heads:
        q = q_ref[:, lanes_of(hh)]
        zero = jnp.zeros_like(q)
        qm_ref[hh, 0] = jnp.where(lane < HEAD_DK, q, zero)
        qm_ref[hh, 1] = jnp.where(lane >= HEAD_DK, q, zero)

    def qk_scores(hh, s_ref, mx_ref, tile, diagonal=False):
        kj = k_ref[pl.ds(pl.multiple_of(tile * t, t), t), lanes_of(hh)]
        if diagonal:
            bias = bias_ref[hh, t:, :]
        else:
            corner = jnp.where(tile == qi - 1, bias_ref[hh, t - LANES:t, :LANES], 0.0)
        nt = (((1,), (1,)), ((), ()))
        half = t // 2
        for mi in range(2):
            if diagonal:
                early = lax.dot_general(kj[:half], qm_ref[hh, mi, :half], nt,
                                        preferred_element_type=F32)
                late = lax.dot_general(kj, qm_ref[hh, mi, half:], nt,
                                       preferred_element_type=F32)
                early = jnp.concatenate([early, jnp.zeros((half, half), F32)], axis=0)
                s = jnp.concatenate([early, late], axis=1) + bias
            else:
                s = lax.dot_general(kj, qm_ref[hh, mi], nt,
                                    preferred_element_type=F32)
                low = jnp.concatenate([s[t - LANES:, :LANES] + corner, s[t - LANES:, LANES:]],
                                      axis=1)
                s = jnp.concatenate([s[:t - LANES], low], axis=0)
            s_ref[hh, mi] = s
            mx_ref[hh, mi] = jnp.broadcast_to(jnp.max(s, axis=0, keepdims=True), (SUBLANES, t))

    def first_tile(hh, s_ref, mx_ref, tile):
        vt = vt_ref[hh, tile]
        for mi in range(2):
            m_new = mx_ref[hh, mi]
            p = jnp.exp2(s_ref[hh, mi] - m_new[:1]).astype(BF16)
            m_ref[hh, mi] = m_new
            half = t // 2
            early = jnp.dot(vt[:, :half], p[:half, :half], preferred_element_type=F32)
            late = jnp.dot(vt, p[:, half:], preferred_element_type=F32)
            acc_ref[hh, mi] = jnp.concatenate([early, late], axis=1)

    def next_tile(hh, s_ref, mx_ref, tile):
        vt = vt_ref[hh, tile]
        for mi in range(2):
            m_prev = m_ref[hh, mi]
            m_new = jnp.maximum(m_prev, mx_ref[hh, mi])
            alpha = jnp.exp2(m_prev - m_new)
            p = jnp.exp2(s_ref[hh, mi] - m_new[:1])
            m_ref[hh, mi] = m_new
            acc_ref[hh, mi] = (alpha[:1] * acc_ref[hh, mi]
                               + jnp.dot(vt, p.astype(BF16), preferred_element_type=F32))

    lam_init = scal_ref[0]
    lam = (jnp.exp(jnp.sum(lq1_ref[...] * lk1_ref[...], axis=1, keepdims=True))
           - jnp.exp(jnp.sum(lq2_ref[...] * lk2_ref[...], axis=1, keepdims=True))
           + lam_init)

    n_prev = qi
    last = jnp.maximum(n_prev - 1, 0)
    for hh in heads:
        qk_scores(hh, sa_ref, mxa_ref, qi, diagonal=True)
        qk_scores(hh, sb_ref, mxb_ref, 0)
        first_tile(hh, sa_ref, mxa_ref, qi)

        def two_tiles(first, hh=hh):
            qk_scores(hh, sa_ref, mxa_ref, jnp.minimum(first + 1, last))
            next_tile(hh, sb_ref, mxb_ref, first)
            qk_scores(hh, sb_ref, mxb_ref, jnp.minimum(first + 2, last))
            next_tile(hh, sa_ref, mxa_ref, first + 1)

        def run(start, count, pairs):
            def body(i, carry):
                for j in range(pairs):
                    two_tiles(start + 2 * pairs * i + 2 * j)
                return carry

            lax.fori_loop(0, count, body, 0)

        n8, n4 = n_prev // 8, (n_prev % 8) // 4
        run(0, n8, 4)
        run(8 * n8, n4, 2)
        run(8 * n8 + 4 * n4, (n_prev % 4) // 2, 1)

        @pl.when(n_prev % 2 == 1)
        def _(hh=hh):
            next_tile(hh, sb_ref, mxb_ref, last)

        o_t = (acc_ref[hh, 0, :HEAD_DV, :] / acc_ref[hh, 0, HEAD_DV:HEAD_DV + 1, :]
               - lam * (acc_ref[hh, 1, :HEAD_DV, :] / acc_ref[hh, 1, HEAD_DV:HEAD_DV + 1, :]))
        o = o_t.T
        o = o * lax.rsqrt(jnp.mean(o * o, axis=-1, keepdims=True) + EPS) * sg_ref[...]
        o = o * (1.0 - lam_init)
        z = z_ref[:, lanes_of(hh)].astype(F32)
        o_ref[:, lanes_of(hh)] = (o * (z * jax.nn.sigmoid(z))).astype(o_ref.dtype)


def _attention(proj, scal, rel_bias, lq1, lk1, lq2, lk2, subln_g):
    bsz, seq, _ = proj.shape
    t = ATT_T
    nh = ATT_HEADS
    width = nh * HEAD_DV
    groups = N_HEADS // nh
    smem = pl.BlockSpec(memory_space=pltpu.SMEM)
    vec = lambda n: pl.BlockSpec((1, n), lambda g, b, i: (0, 0))
    return pl.pallas_call(
        _attn_kernel,
        grid=(groups, bsz, seq // t),
        in_specs=[smem, smem, vec(HEAD_DK), vec(HEAD_DK), vec(HEAD_DK), vec(HEAD_DK),
                  vec(HEAD_DV),
                  pl.BlockSpec((None, t, width), lambda g, b, i: (b, i, g)),
                  pl.BlockSpec((None, seq, width), lambda g, b, i: (b, 0, groups + g)),
                  pl.BlockSpec((None, seq, width), lambda g, b, i: (b, 0, 2 * groups + g)),
                  pl.BlockSpec((None, t, width), lambda g, b, i: (b, i, 3 * groups + g))],
        out_specs=pl.BlockSpec((None, t, width), lambda g, b, i: (b, i, g)),
        out_shape=jax.ShapeDtypeStruct((bsz, seq, D_ATT), BF16),
        scratch_shapes=[pltpu.VMEM((nh, 2 * t, t), F32),
                        pltpu.VMEM((nh, 2, SUBLANES, t), F32),
                        pltpu.VMEM((nh, 2, VT_ROWS, t), F32),
                        pltpu.VMEM((nh, seq // t, VT_ROWS, t), BF16),
                        pltpu.VMEM((nh, 2, t, HEAD_DV), BF16),
                        pltpu.VMEM((nh, 2, t, t), F32),
                        pltpu.VMEM((nh, 2, t, t), F32),
                        pltpu.VMEM((nh, 2, SUBLANES, t), F32),
                        pltpu.VMEM((nh, 2, SUBLANES, t), F32)],
        compiler_params=pltpu.CompilerParams(
            dimension_semantics=("arbitrary", "arbitrary", "arbitrary"),
            vmem_limit_bytes=VMEM_LIMIT),
        name="diff_attention",
    )(scal, rel_bias, lq1, lk1, lq2, lk2, subln_g, proj, proj, proj, proj)


def _gelu_tanh(x):
    return 0.5 * x * (1.0 + jnp.tanh(math.sqrt(2.0 / math.pi) * (x + 0.044715 * (x * x * x))))


def _ssm_kernel(u_ref, z_ref, bw_ref, cw_ref, are_ref, aim_ref, d_ref, wglu_ref,
                o_ref, hre_ref, him_ref, st_ref, g_ref):
    nb = u_ref.shape[0]
    tb = SSM_TB
    pitch = SSM_PITCH
    pairs_per_chunk = MXU_DIM // (2 * SSM_GROUP)
    n_vregs = N_PAIRS // SUBLANES

    def pair_rows(b, gp):
        return slice((b * N_PAIRS + gp) * pitch, (b * N_PAIRS + gp) * pitch + tb)

    @pl.when(pl.program_id(0) == 0)
    def _():
        st_ref[...] = jnp.zeros_like(st_ref)

    for gp in range(N_PAIRS):
        kc = gp // pairs_per_chunk
        uc = jnp.concatenate([u_ref[b, :, kc * MXU_DIM:(kc + 1) * MXU_DIM] for b in range(nb)],
                             axis=0)
        bu = jnp.dot(uc, bw_ref[gp], preferred_element_type=F32)
        for b in range(nb):
            hre_ref[pair_rows(b, gp), :] = bu[b * tb:(b + 1) * tb, :LANES]
            him_ref[pair_rows(b, gp), :] = bu[b * tb:(b + 1) * tb, LANES:]

    a_re = [are_ref[k] for k in range(n_vregs)]
    a_im = [aim_ref[k] for k in range(n_vregs)]

    def step(ti, carry):
        out = []
        for b in range(nb):
            for k in range(n_vregs):
                h_re, h_im = carry[2 * (b * n_vregs + k)], carry[2 * (b * n_vregs + k) + 1]
                rows = pl.ds((b * N_PAIRS + k * SUBLANES) * pitch + ti, SUBLANES, stride=pitch)
                n_re = a_re[k] * h_re - a_im[k] * h_im + hre_ref[rows, :]
                n_im = a_re[k] * h_im + a_im[k] * h_re + him_ref[rows, :]
                hre_ref[rows, :] = n_re
                him_ref[rows, :] = n_im
                out += [n_re, n_im]
        return tuple(out)

    init = tuple(st_ref[i] for i in range(2 * nb * n_vregs))
    final = lax.fori_loop(0, tb, step, init, unroll=SCAN_UNROLL)
    for i, s in enumerate(final):
        st_ref[i] = s

    for kc in range(D_SSM // MXU_DIM):
        gps = range(kc * pairs_per_chunk, (kc + 1) * pairs_per_chunk)
        hcat = jnp.concatenate(
            [jnp.concatenate([ref[pair_rows(b, gp), :]
                              for gp in gps for ref in (hre_ref, him_ref)], axis=1)
             for b in range(nb)], axis=0).astype(BF16)
        cwk = cw_ref[kc * pairs_per_chunk:(kc + 1) * pairs_per_chunk]
        acc = jnp.dot(hcat, cwk.reshape(pairs_per_chunk * MXU_DIM, MXU_DIM),
                      preferred_element_type=F32)
        cols = slice(kc * MXU_DIM, (kc + 1) * MXU_DIM)
        for b in range(nb):
            y = acc[b * tb:(b + 1) * tb] + d_ref[:, cols] * u_ref[b, :, cols].astype(F32)
            g_ref[b * tb:(b + 1) * tb, cols] = _gelu_tanh(y).astype(BF16)

    for kc in range(D_SSM // MXU_DIM):
        cols = slice(kc * MXU_DIM, (kc + 1) * MXU_DIM)
        gate_cols = slice(D_SSM + kc * MXU_DIM, D_SSM + (kc + 1) * MXU_DIM)
        val = jnp.dot(g_ref[...], wglu_ref[:, cols], preferred_element_type=F32)
        gate = jnp.dot(g_ref[...], wglu_ref[:, gate_cols], preferred_element_type=F32)
        for b in range(nb):
            rows = slice(b * tb, (b + 1) * tb)
            z = z_ref[b, :, cols].astype(F32)
            o = val[rows] * jax.nn.sigmoid(gate[rows]) * (z * jax.nn.sigmoid(z))
            o_ref[b, :, cols] = o.astype(o_ref.dtype)


def _ssm(proj, bw, cw, a_re_v, a_im_v, d_skip, w_glu):
    bsz, seq, _ = proj.shape
    tb = SSM_TB
    const = lambda shape: pl.BlockSpec(shape, lambda i: (0,) * len(shape),
                                       pipeline_mode=pl.Buffered(1))
    u_col = 4 * D_ATT // D_SSM
    n_state_vregs = 2 * bsz * (N_PAIRS // SUBLANES)
    return pl.pallas_call(
        _ssm_kernel,
        grid=(seq // tb,),
        in_specs=[pl.BlockSpec((bsz, tb, D_SSM), lambda i: (0, i, u_col)),
                  pl.BlockSpec((bsz, tb, D_SSM), lambda i: (0, i, u_col + 1)),
                  const((N_PAIRS, MXU_DIM, MXU_DIM)),
                  const((N_PAIRS, MXU_DIM, MXU_DIM)),
                  const((N_PAIRS // SUBLANES, SUBLANES, LANES)),
                  const((N_PAIRS // SUBLANES, SUBLANES, LANES)),
                  const((1, D_SSM)),
                  const((D_SSM, 2 * D_SSM))],
        out_specs=pl.BlockSpec((bsz, tb, D_SSM), lambda i: (0, i, 0)),
        out_shape=jax.ShapeDtypeStruct((bsz, seq, D_SSM), BF16),
        scratch_shapes=[pltpu.VMEM((bsz * N_PAIRS * SSM_PITCH, LANES), F32),
                        pltpu.VMEM((bsz * N_PAIRS * SSM_PITCH, LANES), F32),
                        pltpu.VMEM((n_state_vregs, SUBLANES, LANES), F32),
                        pltpu.VMEM((bsz * tb, D_SSM), BF16)],
        compiler_params=pltpu.CompilerParams(
            dimension_semantics=("arbitrary",),
            vmem_limit_bytes=VMEM_LIMIT),
        name="s5_glu",
    )(proj, proj, bw, cw, a_re_v, a_im_v, d_skip, w_glu)


def _ssm_weights(ab_re, ab_im, bb_re, bb_im, c_re, c_im):
    ppc = MXU_DIM // (2 * SSM_GROUP)
    sel = (jnp.arange(N_PAIRS)[:, None] % ppc == jnp.arange(ppc)[None, :]).astype(F32)
    group = jnp.arange(2)[None, :, None, None]

    def place(x_re, x_im):
        pieces = []
        for x in (x_re, x_im):
            xg = x.reshape(N_PAIRS, 2, SSM_GROUP, STATE)
            for gl2 in range(2):
                piece = jnp.where(group == gl2, xg, 0.0)[:, None] * sel[:, :, None, None, None]
                pieces.append(piece.reshape(N_PAIRS, ppc, 2 * SSM_GROUP, STATE))
        w = jnp.concatenate(pieces, axis=-1)
        return w.reshape(N_PAIRS, MXU_DIM, MXU_DIM)

    bw = place(bb_re, bb_im).astype(BF16)
    cw = jnp.swapaxes(place(c_re, -c_im), 1, 2).astype(BF16)
    vshape = (N_PAIRS // SUBLANES, SUBLANES, LANES)
    return bw, cw, ab_re.reshape(vshape), ab_im.reshape(vshape)


def _out_proj_kernel(oa_ref, os_ref, w_ref, g_ref, x_ref, o_ref):
    mix = (jnp.dot(oa_ref[...], w_ref[:D_ATT, :], preferred_element_type=F32)
           + jnp.dot(os_ref[...], w_ref[D_ATT:, :], preferred_element_type=F32))
    y = mix * lax.rsqrt(jnp.mean(mix * mix, axis=-1, keepdims=True) + EPS)
    o_ref[...] = x_ref[...] + y * g_ref[...]


def _out_proj(o_att, o_ssm, w, g, x2d):
    m = x2d.shape[0]
    return pl.pallas_call(
        _out_proj_kernel,
        grid=(m // OUT_TM,),
        in_specs=[pl.BlockSpec((OUT_TM, D_ATT), lambda i: (i, 0)),
                  pl.BlockSpec((OUT_TM, D_SSM), lambda i: (i, 0)),
                  pl.BlockSpec((D_MODEL, D_MODEL), lambda i: (0, 0),
                               pipeline_mode=pl.Buffered(1)),
                  pl.BlockSpec((1, D_MODEL), lambda i: (0, 0)),
                  pl.BlockSpec((OUT_TM, D_MODEL), lambda i: (i, 0))],
        out_specs=pl.BlockSpec((OUT_TM, D_MODEL), lambda i: (i, 0)),
        out_shape=jax.ShapeDtypeStruct((m, D_MODEL), F32),
        compiler_params=pltpu.CompilerParams(
            dimension_semantics=("arbitrary",),
            vmem_limit_bytes=VMEM_LIMIT),
        name="out_proj",
    )(o_att, o_ssm, w, g, x2d)


def kernel(x, rel_bias, pre_norm_g, post_norm_g, w_in, lambda_q1, lambda_k1, lambda_q2,
           lambda_k2, subln_g, ssm_a_re, ssm_a_im, ssm_log_dt, ssm_b_re, ssm_b_im,
           ssm_c_re, ssm_c_im, ssm_d, w_glu, w_out):
    bsz, seq, _ = x.shape
    m = bsz * seq
    ab_re, ab_im, bb_re, bb_im = _discretize(ssm_a_re, ssm_a_im, ssm_log_dt, ssm_b_re, ssm_b_im)
    col_scale = jnp.concatenate([jnp.full((D_ATT,), HEAD_DK ** -0.5 * LOG2E, F32),
                                 jnp.ones((D_IN - D_ATT,), F32)])
    x2d = x.reshape(m, D_MODEL)
    for l in range(DEPTH):
        lam_init = _lambda_init(l)
        w_in_l = (w_in[l] * col_scale).astype(BF16)
        proj = _in_proj(x2d, pre_norm_g[l].reshape(1, D_MODEL), w_in_l)
        proj = proj.reshape(bsz, seq, D_IN)
        scal = jnp.array([lam_init], F32)
        o_att = _attention(proj, scal, rel_bias,
                           lambda_q1[l].reshape(1, HEAD_DK), lambda_k1[l].reshape(1, HEAD_DK),
                           lambda_q2[l].reshape(1, HEAD_DK), lambda_k2[l].reshape(1, HEAD_DK),
                           subln_g[l].reshape(1, HEAD_DV))
        bw, cw, a_re_v, a_im_v = _ssm_weights(
            ab_re[l, :, 0, :], ab_im[l, :, 0, :], bb_re[l], bb_im[l], ssm_c_re[l], ssm_c_im[l])
        o_ssm = _ssm(proj, bw, cw, a_re_v, a_im_v, ssm_d[l].reshape(1, D_SSM),
                     w_glu[l].astype(BF16))
        x2d = _out_proj(o_att.reshape(m, D_ATT), o_ssm.reshape(m, D_SSM),
                        w_out[l].astype(BF16), post_norm_g[l].reshape(1, D_MODEL), x2d)
    return x2d.reshape(bsz, seq, D_MODEL)
```
